```python
import math
import jax, jax.numpy as jnp
from jax import lax
import numpy as np

D_MODEL = 1024
BATCH = 16
SEQ = 2048
DEPTH = 2

EPS = 1e-6
D_FF = 4 * D_MODEL
CONV_CH = D_MODEL // 2
CONV_WIDTH = 31
ATTN_HEAD_DIM = 64
ATTN_HEADS = (D_MODEL // 2) // ATTN_HEAD_DIM
ATTN_WIDTH = ATTN_HEADS * ATTN_HEAD_DIM
MOBA_BLOCK = 256
MOBA_TOP_K = 3
MOBA_Q_CHUNK = 32
AB_IN_WIDTH = 2 * CONV_CH + 3 * ATTN_WIDTH
AB_OUT_WIDTH = CONV_CH + ATTN_WIDTH
LRU_WIDTH = D_MODEL
LRU_HEADS = 8
LRU_BLOCK = LRU_WIDTH // LRU_HEADS
LRU_CONV_WIDTH = 4
LRU_C = 8.0
N_EVEN = (DEPTH + 1) // 2
N_ODD = DEPTH // 2

kernel_name = "hybrid_conformer_moba_rglru_block"


def rmsnorm(x, g):
    xf = x.astype(jnp.float32)
    y = xf * lax.rsqrt(jnp.mean(xf * xf, axis=-1, keepdims=True) + EPS)
    return (y * g.astype(jnp.float32)).astype(x.dtype)


def layernorm(x, g, b):
    xf = x.astype(jnp.float32)
    mu = jnp.mean(xf, axis=-1, keepdims=True)
    var = jnp.mean(jnp.square(xf - mu), axis=-1, keepdims=True)
    y = (xf - mu) * lax.rsqrt(var + EPS)
    return (y * g.astype(jnp.float32) + b.astype(jnp.float32)).astype(x.dtype)


def causal_depthwise_conv(x, w):
    width, ch = w.shape
    return lax.conv_general_dilated(
        x, w[:, None, :].astype(x.dtype), window_strides=(1,),
        padding=[(width - 1, 0)], dimension_numbers=("NWC", "WIO", "NWC"),
        feature_group_count=ch)


def alibi_slopes(n_heads):
    return 2.0 ** (-8.0 * jnp.arange(1, n_heads + 1, dtype=jnp.float32) / n_heads)


def conformer_conv(u_val, u_gate, conv_w, ln_g, ln_b):
    z = u_val * jax.nn.sigmoid(u_gate)
    z = causal_depthwise_conv(z, conv_w)
    z = layernorm(z, ln_g, ln_b)
    return jax.nn.silu(z)


def moba_attention(q, k, v):
    b, s, h, dh = q.shape
    nb = -(-s // MOBA_BLOCK)
    sp = nb * MOBA_BLOCK
    pad = sp - s
    q, k, v = [jnp.pad(t.transpose(0, 2, 1, 3), ((0, 0), (0, 0), (0, pad), (0, 0)))
               for t in (q, k, v)]
    slopes = alibi_slopes(h)[None, :, None, None]
    scale = dh ** -0.5
    n_past = min(MOBA_TOP_K, nb - 1)
    offs = jnp.arange(MOBA_BLOCK)
    kb = k.reshape(b, h, nb, MOBA_BLOCK, dh)
    vb = v.reshape(b, h, nb, MOBA_BLOCK, dh)
    bi = jnp.arange(b)[:, None, None, None]
    hi = jnp.arange(h)[None, :, None, None]

    if n_past > 0:
        kbar = jnp.mean(kb.astype(jnp.float32), axis=3)
        gate = jnp.einsum("bhsd,bhnd->bhsn", q.astype(jnp.float32), kbar)
        qblk = jnp.arange(sp) // MOBA_BLOCK
        past = jnp.arange(nb)[None, :] < qblk[:, None]
        gate = jnp.where(past, gate, -jnp.inf)
        top_val, top_idx = lax.top_k(gate, n_past)
        top_ok = jnp.isfinite(top_val)

    def chunk(c):
        start = c * MOBA_Q_CHUNK
        t = start + jnp.arange(MOBA_Q_CHUNK)
        qc = lax.dynamic_slice_in_dim(q, start, MOBA_Q_CHUNK, axis=2)
        own0 = (start // MOBA_BLOCK) * MOBA_BLOCK
        k_own = lax.dynamic_slice_in_dim(k, own0, MOBA_BLOCK, axis=2)
        v_own = lax.dynamic_slice_in_dim(v, own0, MOBA_BLOCK, axis=2)
        dist_own = (t[:, None] - (own0 + offs)[None, :]).astype(jnp.float32)
        l_own = (jnp.einsum("bhqd,bhkd->bhqk", qc, k_own).astype(jnp.float32) * scale
                 - slopes * dist_own)
        l_own = jnp.where(dist_own >= 0, l_own, -jnp.inf)
        if n_past > 0:
            idx = lax.dynamic_slice_in_dim(top_idx, start, MOBA_Q_CHUNK, axis=2)
            ok = lax.dynamic_slice_in_dim(top_ok, start, MOBA_Q_CHUNK, axis=2)
            k_sel = kb[bi, hi, idx]
            v_sel = vb[bi, hi, idx]
            pos = idx[..., None] * MOBA_BLOCK + offs
            dist = (t[None, None, :, None, None] - pos).astype(jnp.float32)
            l_sel = (jnp.einsum("bhqd,bhqnkd->bhqnk", qc, k_sel).astype(jnp.float32) * scale
                     - slopes[..., None] * dist)
            l_sel = jnp.where(ok[..., None], l_sel, -jnp.inf)
            n_sel = n_past * MOBA_BLOCK
            logits = jnp.concatenate(
                [l_sel.reshape(b, h, MOBA_Q_CHUNK, n_sel), l_own], axis=-1)
            p = jax.nn.softmax(logits, axis=-1).astype(v.dtype)
            p_sel = p[..., :n_sel].reshape(b, h, MOBA_Q_CHUNK, n_past, MOBA_BLOCK)
            p_own = p[..., n_sel:]
            o = (jnp.einsum("bhqnk,bhqnkd->bhqd", p_sel, v_sel)
                 + jnp.einsum("bhqk,bhkd->bhqd", p_own, v_own))
        else:
            p_own = jax.nn.softmax(l_own, axis=-1).astype(v.dtype)
            o = jnp.einsum("bhqk,bhkd->bhqd", p_own, v_own)
        return o

    out = lax.map(chunk, jnp.arange(sp // MOBA_Q_CHUNK))
    out = out.transpose(1, 2, 0, 3, 4).reshape(b, h, sp, dh)[:, :, :s]
    return out.transpose(0, 2, 1, 3)


def rg_lru(x, w_r, b_r, w_i, b_i, lam):
    b, s, w = x.shape
    xb = x.reshape(b, s, LRU_HEADS, LRU_BLOCK)
    r = jax.nn.sigmoid(jnp.einsum("bshi,hij->bshj", xb, w_r) + b_r).reshape(b, s, w)
    i = jax.nn.sigmoid(jnp.einsum("bshi,hij->bshj", xb, w_i) + b_i).reshape(b, s, w)
    log_a = -LRU_C * r.astype(jnp.float32) * jax.nn.softplus(-lam.astype(jnp.float32))
    a = jnp.exp(log_a)
    mult = jnp.sqrt(-jnp.expm1(2.0 * log_a))
    bx = mult * (i * x).astype(jnp.float32)

    def combine(e1, e2):
        a1, b1 = e1
        a2, b2 = e2
        return a1 * a2, a2 * b1 + b2

    _, hseq = lax.associative_scan(combine, (a, bx), axis=1)
    return hseq.astype(x.dtype)


def mixer_ab(h, w_in, conv_w, ln_g, ln_b, w_out):
    b, s, _ = h.shape
    u = h @ w_in
    u_val, u_gate, q, k, v = jnp.split(
        u, np.cumsum([CONV_CH, CONV_CH, ATTN_WIDTH, ATTN_WIDTH]).tolist(), axis=-1)
    ya = conformer_conv(u_val, u_gate, conv_w, ln_g, ln_b)
    shp = (b, s, ATTN_HEADS, ATTN_HEAD_DIM)
    yb = moba_attention(q.reshape(shp), k.reshape(shp), v.reshape(shp)).reshape(b, s, ATTN_WIDTH)
    return jnp.concatenate([ya, yb], axis=-1) @ w_out


def mixer_c(h, w_in, conv_w, conv_b, w_r, b_r, w_i, b_i, lam, w_out):
    u = h @ w_in
    gate, xr = jnp.split(u, 2, axis=-1)
    xr = causal_depthwise_conv(xr, conv_w) + conv_b
    y = rg_lru(xr, w_r, b_r, w_i, b_i, lam) * jax.nn.gelu(gate)
    return y @ w_out


def sq_relu_mlp(h, w_up, w_down):
    return jnp.square(jax.nn.relu(h @ w_up)) @ w_down


def setup_inputs(seed: int = 0) -> dict:
    key = jax.random.key(seed)
    ks = iter(jax.random.split(key, 32))
    f32 = jnp.float32

    def nrm(shape, fan_in):
        return jax.random.normal(next(ks), shape, f32) * (fan_in ** -0.5)

    def gain(shape):
        return 1.0 + 0.02 * jax.random.normal(next(ks), shape, f32)

    def bias(shape):
        return 0.01 * jax.random.normal(next(ks), shape, f32)

    x = jax.random.normal(next(ks), (BATCH, SEQ, D_MODEL), f32)
    a_c = jax.random.uniform(next(ks), (N_ODD, LRU_WIDTH), f32, 0.9, 0.999)
    a_base = a_c ** (1.0 / LRU_C)
    lam = jnp.log(a_base) - jnp.log1p(-a_base)
    return {
        "x": x,
        "mix_norm": gain((DEPTH, D_MODEL)),
        "mlp_norm": gain((DEPTH, D_MODEL)),
        "w_up": nrm((DEPTH, D_MODEL, D_FF), D_MODEL),
        "w_down": nrm((DEPTH, D_FF, D_MODEL), D_FF),
        "ab_w_in": nrm((N_EVEN, D_MODEL, AB_IN_WIDTH), D_MODEL),
        "ab_conv_w": nrm((N_EVEN, CONV_WIDTH, CONV_CH), CONV_WIDTH),
        "ab_ln_g": gain((N_EVEN, CONV_CH)),
        "ab_ln_b": bias((N_EVEN, CONV_CH)),
        "ab_w_out": nrm((N_EVEN, AB_OUT_WIDTH, D_MODEL), AB_OUT_WIDTH),
        "c_w_in": nrm((N_ODD, D_MODEL, 2 * LRU_WIDTH), D_MODEL),
        "c_conv_w": nrm((N_ODD, LRU_CONV_WIDTH, LRU_WIDTH), LRU_CONV_WIDTH),
        "c_conv_b": bias((N_ODD, LRU_WIDTH)),
        "c_w_r": nrm((N_ODD, LRU_HEADS, LRU_BLOCK, LRU_BLOCK), LRU_BLOCK),
        "c_b_r": bias((N_ODD, LRU_HEADS, LRU_BLOCK)),
        "c_w_i": nrm((N_ODD, LRU_HEADS, LRU_BLOCK, LRU_BLOCK), LRU_BLOCK),
        "c_b_i": bias((N_ODD, LRU_HEADS, LRU_BLOCK)),
        "c_lambda": lam,
        "c_w_out": nrm((N_ODD, LRU_WIDTH, D_MODEL), LRU_WIDTH),
        "final_norm": gain((D_MODEL,)),
    }


def reference(x, mix_norm, mlp_norm, w_up, w_down, ab_w_in, ab_conv_w, ab_ln_g, ab_ln_b,
              ab_w_out, c_w_in, c_conv_w, c_conv_b, c_w_r, c_b_r, c_w_i, c_b_i, c_lambda,
              c_w_out, final_norm):
    for layer in range(DEPTH):
        j = layer // 2
        h = rmsnorm(x, mix_norm[layer])
        if layer % 2 == 0:
            y = mixer_ab(h, ab_w_in[j], ab_conv_w[j], ab_ln_g[j], ab_ln_b[j], ab_w_out[j])
        else:
            y = mixer_c(h, c_w_in[j], c_conv_w[j], c_conv_b[j], c_w_r[j], c_b_r[j],
                        c_w_i[j], c_b_i[j], c_lambda[j], c_w_out[j])
        x = x + y
        x = x + sq_relu_mlp(rmsnorm(x, mlp_norm[layer]), w_up[layer], w_down[layer])
    return rmsnorm(x, final_norm)
```

```python
import functools

import jax
import jax.numpy as jnp
from jax import lax
from jax.experimental import pallas as pl
from jax.experimental.pallas import tpu as pltpu

F32 = jnp.float32
BF16 = jnp.bfloat16

EPS = 1e-6
D_MODEL = 1024
D_FF = 4 * D_MODEL
CONV_CH = 512
CONV_WIDTH = 31
ATTN_WIDTH = 512
HEAD_DIM = 64
N_HEADS = 8
MOBA_BLOCK = 256
MOBA_TOP_K = 3
LRU_HEADS = 8
LRU_BLOCK = 128
LRU_CONV_WIDTH = 4
LRU_C = 8.0

LANES = 128
HEADS_PER_STEP = LANES // HEAD_DIM
MASKED = -1e30
VMEM_LIMIT = 56 * 1024 * 1024

ROW_TILE = 512
MLP_CHUNK = 1024
CONV_ROWS = 64


def _dot(a, b):
    return jnp.dot(a, b, preferred_element_type=F32)


def _dot_nt(a, b):
    return lax.dot_general(a, b, (((1,), (1,)), ((), ())), preferred_element_type=F32)


def _rmsnorm(x, g):
    return x * lax.rsqrt(jnp.mean(x * x, axis=-1, keepdims=True) + EPS) * g


def _const_spec(shape):
    zeros = (0,) * len(shape)
    return pl.BlockSpec(shape, lambda *_: zeros, pipeline_mode=pl.Buffered(1))


def _params(n_axes):
    return pltpu.CompilerParams(dimension_semantics=("arbitrary",) * n_axes,
                                vmem_limit_bytes=VMEM_LIMIT)


def _in0_kernel(x_ref, g_ref, w_ref, z_ref, q_ref, k_ref, v_ref):
    h = _rmsnorm(x_ref[...], g_ref[...]).astype(BF16)
    u_val = _dot(h, w_ref[:, 0:CONV_CH])
    u_gate = _dot(h, w_ref[:, CONV_CH:2 * CONV_CH])
    z_ref[...] = u_val * jax.nn.sigmoid(u_gate)
    o = 2 * CONV_CH
    q_ref[...] = _dot(h, w_ref[:, o:o + ATTN_WIDTH]).astype(BF16)
    k_ref[...] = _dot(h, w_ref[:, o + ATTN_WIDTH:o + 2 * ATTN_WIDTH]).astype(BF16)
    v_ref[...] = _dot(h, w_ref[:, o + 2 * ATTN_WIDTH:o + 3 * ATTN_WIDTH]).astype(BF16)


def _in0(x, g, w):
    rows = x.shape[0]
    n_out = w.shape[1]
    row = lambda width: pl.BlockSpec((ROW_TILE, width), lambda i: (i, 0))
    return pl.pallas_call(
        _in0_kernel,
        grid=(rows // ROW_TILE,),
        in_specs=[row(D_MODEL), _const_spec((1, D_MODEL)), _const_spec((D_MODEL, n_out))],
        out_specs=[row(CONV_CH), row(ATTN_WIDTH), row(ATTN_WIDTH), row(ATTN_WIDTH)],
        out_shape=[jax.ShapeDtypeStruct((rows, CONV_CH), F32)]
        + [jax.ShapeDtypeStruct((rows, ATTN_WIDTH), BF16)] * 3,
        compiler_params=_params(1),
        name="in_proj0",
    )(x, g, w)


def _head_lanes(hh):
    qk0 = hh * HEAD_DIM
    ex0 = (1 - hh) * HEAD_DIM
    return qk0, ex0, ex0 + 16


def _attn_prepare(k_ref, slope_ref, kaug_ref, kbar_ref):
    n_keys = k_ref.shape[0]
    k = k_ref[...]
    rowk = lax.broadcasted_iota(jnp.int32, (n_keys, LANES), 0)
    lane = lax.broadcasted_iota(jnp.int32, (n_keys, LANES), 1)
    tk = (rowk & (MOBA_BLOCK - 1)).astype(F32)
    blk = rowk >> (MOBA_BLOCK.bit_length() - 1)
    kbar_ref[...] = jnp.zeros(kbar_ref.shape, F32)
    lane1 = lax.broadcasted_iota(jnp.int32, (1, LANES), 1)
    for hh in range(HEADS_PER_STEP):
        qk0, ex0, bias0 = _head_lanes(hh)
        slope = slope_ref[hh:hh + 1, :]
        extra = jnp.where(lane == ex0, 1.0,
                          jnp.where(lane == ex0 + 1, slope * tk,
                                    jnp.where(lane - bias0 == blk, 1.0, 0.0)))
        is_k = (lane >= qk0) & (lane < qk0 + HEAD_DIM)
        kaug_ref[hh] = jnp.where(is_k, k, extra.astype(BF16))
        is_k1 = (lane1 >= qk0) & (lane1 < qk0 + HEAD_DIM)
        for j in range(n_keys // MOBA_BLOCK):
            kj = k_ref[j * MOBA_BLOCK:(j + 1) * MOBA_BLOCK, :].astype(F32)
            kbar = jnp.sum(kj, axis=0, keepdims=True) * (1.0 / MOBA_BLOCK)
            kbar_ref[hh, bias0 + j:bias0 + j + 1, :] = jnp.where(is_k1, kbar, 0.0)


def _attn_block(c, q_ref, v_ref, slope_ref, kaug_ref, kbar_ref, o_ref):
    n_past = c * MOBA_BLOCK
    q = q_ref[...]
    row = lax.broadcasted_iota(jnp.int32, (MOBA_BLOCK, LANES), 0)
    lane = lax.broadcasted_iota(jnp.int32, (MOBA_BLOCK, LANES), 1)
    lane_f = lane.astype(F32)
    tq = row.astype(F32)
    causal = (lax.broadcasted_iota(jnp.int32, (MOBA_BLOCK, MOBA_BLOCK), 0)
              >= lax.broadcasted_iota(jnp.int32, (MOBA_BLOCK, MOBA_BLOCK), 1))
    outs = []
    for hh in range(HEADS_PER_STEP):
        qk0, ex0, bias0 = _head_lanes(hh)
        slope = slope_ref[hh:hh + 1, :]
        jrel = lane - bias0
        elig = (jrel >= 0) & (jrel < c)
        if c > 0:
            kbar = kbar_ref[hh]
            kb_hi = kbar.astype(BF16)
            kb_lo = (kbar - kb_hi.astype(F32)).astype(BF16)
            gate = _dot_nt(q, kb_hi) + _dot_nt(q, kb_lo)
            finite = jnp.abs(gate) < jnp.inf
            if c <= MOBA_TOP_K:
                sel = elig & finite
            else:
                g = jnp.where(elig & finite, gate, -jnp.inf)
                sel = jnp.zeros(g.shape, jnp.bool_)
                for _ in range(MOBA_TOP_K):
                    m = jnp.max(g, axis=-1, keepdims=True)
                    first = jnp.min(jnp.where(g == m, lane_f, 1e9), axis=-1, keepdims=True)
                    pick = (lane_f == first) & (m > -jnp.inf)
                    sel = sel | pick
                    g = jnp.where(pick, -jnp.inf, g)
            dist = (c - jrel).astype(F32) * float(MOBA_BLOCK)
            bias = jnp.where(elig, jnp.where(sel, -slope * dist, MASKED), 0.0)
        else:
            bias = jnp.zeros((MOBA_BLOCK, LANES), F32)
        extra = jnp.where(lane == ex0, -slope * tq, jnp.where(lane == ex0 + 1, 1.0, bias))
        is_q = (lane >= qk0) & (lane < qk0 + HEAD_DIM)
        qa = jnp.where(is_q, q * jnp.asarray(HEAD_DIM ** -0.5, BF16), extra.astype(BF16))

        l_own = _dot_nt(qa, kaug_ref[hh, n_past:n_past + MOBA_BLOCK, :])
        l_own = jnp.where(causal, l_own, -jnp.inf)
        m = jnp.max(l_own, axis=-1, keepdims=True)
        if c > 0:
            l_past = _dot_nt(qa, kaug_ref[hh, 0:n_past, :])
            m = jnp.maximum(m, jnp.max(l_past, axis=-1, keepdims=True))
        p_own = jnp.exp(l_own - m)
        denom = jnp.sum(p_own, axis=-1, keepdims=True)
        o = _dot(p_own.astype(BF16), v_ref[n_past:n_past + MOBA_BLOCK, :])
        if c > 0:
            p_past = jnp.exp(l_past - m)
            denom = denom + jnp.sum(p_past, axis=-1, keepdims=True)
            o = o + _dot(p_past.astype(BF16), v_ref[0:n_past, :])
        outs.append(o * (1.0 / denom))
    o_ref[...] = jnp.where(lane < HEAD_DIM, outs[0], outs[1]).astype(o_ref.dtype)


def _attn_kernel(slope_ref, q_ref, k_ref, v_ref, o_ref, kaug_ref, kbar_ref):
    i = pl.program_id(2)

    @pl.when(i == 0)
    def _():
        _attn_prepare(k_ref, slope_ref, kaug_ref, kbar_ref)

    for c in range(k_ref.shape[0] // MOBA_BLOCK):
        @pl.when(i == c)
        def _(c=c):
            _attn_block(c, q_ref, v_ref, slope_ref, kaug_ref, kbar_ref, o_ref)


def _attention(q, k, v, batch):
    seq = q.shape[0]
    n_blk = seq // MOBA_BLOCK
    pairs = N_HEADS // HEADS_PER_STEP
    slopes = 2.0 ** (-8.0 * jnp.arange(1, N_HEADS + 1, dtype=F32) / N_HEADS)
    slopes = jnp.broadcast_to(slopes.reshape(pairs, HEADS_PER_STEP, 1), (pairs, HEADS_PER_STEP, LANES))
    slopes = jnp.pad(slopes, ((0, 0), (0, 8 - HEADS_PER_STEP), (0, 0)))
    col = lambda b, hp, i: b * pairs + hp
    return pl.pallas_call(
        _attn_kernel,
        grid=(batch, pairs, n_blk),
        in_specs=[
            pl.BlockSpec((None, 8, LANES), lambda b, hp, i: (hp, 0, 0)),
            pl.BlockSpec((MOBA_BLOCK, LANES), lambda b, hp, i: (i, col(b, hp, i))),
            pl.BlockSpec((seq, LANES), lambda b, hp, i: (0, col(b, hp, i))),
            pl.BlockSpec((seq, LANES), lambda b, hp, i: (0, col(b, hp, i))),
        ],
        out_specs=pl.BlockSpec((MOBA_BLOCK, LANES), lambda b, hp, i: (i, col(b, hp, i))),
        out_shape=jax.ShapeDtypeStruct(q.shape, BF16),
        scratch_shapes=[pltpu.VMEM((HEADS_PER_STEP, seq, LANES), BF16),
                        pltpu.VMEM((HEADS_PER_STEP, LANES, LANES), F32)],
        compiler_params=_params(3),
        name="moba_attention",
    )(slopes, q, k, v)


def _mix0_kernel(z_ref, yb_ref, x_ref, cw_ref, lg_ref, lb_ref, w_ref, o_ref, zbuf_ref, conv_ref,
                 *, batch):
    i = pl.program_id(0)

    @pl.when(i == 0)
    def _():
        zbuf_ref[0:ROW_TILE, :] = jnp.zeros((ROW_TILE, CONV_CH), F32)

    zbuf_ref[ROW_TILE:2 * ROW_TILE, :] = z_ref[...]

    def conv_rows(rb, carry):
        r0 = rb * CONV_ROWS
        acc = jnp.zeros((CONV_ROWS, CONV_CH), F32)
        for j in range(CONV_WIDTH):
            start = pl.multiple_of(r0 + (ROW_TILE - (CONV_WIDTH - 1 - j) * batch), batch)
            acc = acc + zbuf_ref[pl.ds(start, CONV_ROWS), :] * cw_ref[j:j + 1, :]
        conv_ref[pl.ds(pl.multiple_of(r0, CONV_ROWS), CONV_ROWS), :] = acc
        return carry

    lax.fori_loop(0, ROW_TILE // CONV_ROWS, conv_rows, 0)
    zbuf_ref[0:ROW_TILE, :] = zbuf_ref[ROW_TILE:2 * ROW_TILE, :]

    y = conv_ref[...]
    mu = jnp.mean(y, axis=-1, keepdims=True)
    d = y - mu
    var = jnp.mean(d * d, axis=-1, keepdims=True)
    y = d * lax.rsqrt(var + EPS) * lg_ref[...] + lb_ref[...]
    ya = (y * jax.nn.sigmoid(y)).astype(BF16)
    o_ref[...] = (x_ref[...] + _dot(ya, w_ref[0:CONV_CH, :])
                  + _dot(yb_ref[...], w_ref[CONV_CH:CONV_CH + ATTN_WIDTH, :]))


def _mix0(z, yb, x, conv_w, ln_g, ln_b, w_out, batch):
    rows = x.shape[0]
    assert (CONV_WIDTH - 1) * batch <= ROW_TILE
    row = lambda width: pl.BlockSpec((ROW_TILE, width), lambda i: (i, 0))
    return pl.pallas_call(
        functools.partial(_mix0_kernel, batch=batch),
        grid=(rows // ROW_TILE,),
        in_specs=[row(CONV_CH), row(ATTN_WIDTH), row(D_MODEL),
                  _const_spec(conv_w.shape), _const_spec((1, CONV_CH)), _const_spec((1, CONV_CH)),
                  _const_spec(w_out.shape)],
        out_specs=row(D_MODEL),
        out_shape=jax.ShapeDtypeStruct((rows, D_MODEL), F32),
        scratch_shapes=[pltpu.VMEM((2 * ROW_TILE, CONV_CH), F32),
                        pltpu.VMEM((ROW_TILE, CONV_CH), F32)],
        compiler_params=_params(1),
        name="mixer0_tail",
    )(z, yb, x, conv_w, ln_g, ln_b, w_out)


def _mlp_kernel(*refs, final):
    if final:
        x_ref, g_ref, wu_ref, wd_ref, fg_ref, o_ref = refs
    else:
        x_ref, g_ref, wu_ref, wd_ref, o_ref = refs
    x = x_ref[...]
    h = _rmsnorm(x, g_ref[...]).astype(BF16)
    acc = x
    for c in range(D_FF // MLP_CHUNK):
        a = _dot(h, wu_ref[:, c * MLP_CHUNK:(c + 1) * MLP_CHUNK])
        a = jnp.square(jnp.maximum(a, 0.0)).astype(BF16)
        acc = acc + _dot(a, wd_ref[c * MLP_CHUNK:(c + 1) * MLP_CHUNK, :])
    if final:
        acc = _rmsnorm(acc, fg_ref[...])
    o_ref[...] = acc


def _mlp(x, g, w_up, w_down, final_g=None):
    rows = x.shape[0]
    final = final_g is not None
    row = pl.BlockSpec((ROW_TILE, D_MODEL), lambda i: (i, 0))
    in_specs = [row, _const_spec((1, D_MODEL)), _const_spec(w_up.shape), _const_spec(w_down.shape)]
    args = [x, g, w_up, w_down]
    if final:
        in_specs.append(_const_spec((1, D_MODEL)))
        args.append(final_g)
    return pl.pallas_call(
        functools.partial(_mlp_kernel, final=final),
        grid=(rows // ROW_TILE,),
        in_specs=in_specs,
        out_specs=row,
        out_shape=jax.ShapeDtypeStruct((rows, D_MODEL), F32),
        compiler_params=_params(1),
        name="mlp_final" if final else "mlp",
    )(*args)


def _gelu_tanh(x):
    return 0.5 * x * (1.0 + jnp.tanh(0.7978845608028654 * (x + 0.044715 * (x * x * x))))


def _lru_kernel(x_ref, g_ref, win_ref, cw_ref, cb_ref, wri_ref, br_ref, bi_ref, lam_ref, wout_ref,
                o_ref, xbuf_ref, a_ref, b_ref, gate_ref, h_ref, *, batch):
    i = pl.program_id(0)
    halo = xbuf_ref.shape[0] - ROW_TILE

    @pl.when(i == 0)
    def _():
        xbuf_ref[0:halo, :] = jnp.zeros((halo, D_MODEL), F32)
        h_ref[...] = jnp.zeros(h_ref.shape, F32)

    x = x_ref[...]
    h = _rmsnorm(x, g_ref[...]).astype(BF16)
    gate_ref[...] = _gelu_tanh(_dot(h, win_ref[:, 0:D_MODEL]))
    xbuf_ref[halo:halo + ROW_TILE, :] = _dot(h, win_ref[:, D_MODEL:2 * D_MODEL])

    neg_lam = -lam_ref[...]
    softplus = jnp.maximum(neg_lam, 0.0) + jnp.log1p(jnp.exp(-jnp.abs(neg_lam)))
    for hd in range(LRU_HEADS):
        cols = slice(hd * LRU_BLOCK, (hd + 1) * LRU_BLOCK)
        xc = cb_ref[:, cols]
        for j in range(LRU_CONV_WIDTH):
            start = halo - (LRU_CONV_WIDTH - 1 - j) * batch
            xc = xc + xbuf_ref[start:start + ROW_TILE, cols] * cw_ref[j:j + 1, cols]
        ri = _dot(xc.astype(BF16), wri_ref[hd])
        r = jax.nn.sigmoid(ri[:, 0:LRU_BLOCK] + br_ref[:, cols])
        ig = jax.nn.sigmoid(ri[:, LRU_BLOCK:2 * LRU_BLOCK] + bi_ref[:, cols])
        log_a = -LRU_C * r * softplus[:, cols]
        a = jnp.exp(log_a)
        a_ref[:, cols] = a
        b_ref[:, cols] = jnp.sqrt(-jnp.tanh(log_a) * (a * a + 1.0)) * (ig * xc)
    xbuf_ref[0:halo, :] = xbuf_ref[ROW_TILE:ROW_TILE + halo, :]

    state = h_ref[...]
    for t in range(ROW_TILE // batch):
        rows = slice(t * batch, (t + 1) * batch)
        state = a_ref[rows, :] * state + b_ref[rows, :]
        b_ref[rows, :] = state
    h_ref[...] = state

    y = (b_ref[...] * gate_ref[...]).astype(BF16)
    o_ref[...] = x + _dot(y, wout_ref[...])


def _lru(x, g, w_in, conv_w, conv_b, w_ri, b_r, b_i, lam, w_out, batch):
    rows = x.shape[0]
    halo = 64
    assert (LRU_CONV_WIDTH - 1) * batch <= halo
    row = pl.BlockSpec((ROW_TILE, D_MODEL), lambda i: (i, 0))
    vec = _const_spec((1, D_MODEL))
    return pl.pallas_call(
        functools.partial(_lru_kernel, batch=batch),
        grid=(rows // ROW_TILE,),
        in_specs=[row, vec, _const_spec(w_in.shape), _const_spec(conv_w.shape), vec,
                  _const_spec(w_ri.shape), vec, vec, vec, _const_spec(w_out.shape)],
        out_specs=row,
        out_shape=jax.ShapeDtypeStruct((rows, D_MODEL), F32),
        scratch_shapes=[pltpu.VMEM((halo + ROW_TILE, D_MODEL), F32),
                        pltpu.VMEM((ROW_TILE, D_MODEL), F32),
                        pltpu.VMEM((ROW_TILE, D_MODEL), F32),
                        pltpu.VMEM((ROW_TILE, D_MODEL), F32),
                        pltpu.VMEM((batch, D_MODEL), F32)],
        compiler_params=_params(1),
        name="rglru_mixer",
    )(x, g, w_in, conv_w, conv_b, w_ri, b_r, b_i, lam, w_out)


def kernel(x, mix_norm, mlp_norm, w_up, w_down, ab_w_in, ab_conv_w, ab_ln_g, ab_ln_b, ab_w_out,
           c_w_in, c_conv_w, c_conv_b, c_w_r, c_b_r, c_w_i, c_b_i, c_lambda, c_w_out, final_norm):
    batch, seq, d = x.shape
    rows = batch * seq
    vec = lambda a: a.reshape(1, -1).astype(F32)

    xt = x.transpose(1, 0, 2).reshape(rows, d)

    z, q, k, v = _in0(xt, vec(mix_norm[0]), ab_w_in[0].astype(BF16))
    as_cols = lambda a: a.reshape(seq, batch * ATTN_WIDTH)
    yb = _attention(as_cols(q), as_cols(k), as_cols(v), batch).reshape(rows, ATTN_WIDTH)
    conv_w = jnp.pad(ab_conv_w[0], ((0, 32 - CONV_WIDTH), (0, 0)))
    xt = _mix0(z, yb, xt, conv_w, vec(ab_ln_g[0]), vec(ab_ln_b[0]), ab_w_out[0].astype(BF16), batch)
    xt = _mlp(xt, vec(mlp_norm[0]), w_up[0].astype(BF16), w_down[0].astype(BF16))

    w_ri = jnp.concatenate([c_w_r[0], c_w_i[0]], axis=-1).astype(BF16)
    lru_conv_w = jnp.pad(c_conv_w[0], ((0, 8 - LRU_CONV_WIDTH), (0, 0)))
    xt = _lru(xt, vec(mix_norm[1]), c_w_in[0].astype(BF16), lru_conv_w, vec(c_conv_b[0]), w_ri,
              vec(c_b_r[0]), vec(c_b_i[0]), vec(c_lambda[0]), c_w_out[0].astype(BF16), batch)
    xt = _mlp(xt, vec(mlp_norm[1]), w_up[1].astype(BF16), w_down[1].astype(BF16), vec(final_norm))

    return xt.reshape(seq, batch, d).transpose(1, 0, 2)
```

```python
import functools

import jax
import jax.numpy as jnp
from jax import lax
from jax.experimental import pallas as pl
from jax.experimental.pallas import tpu as pltpu

F32 = jnp.float32
BF16 = jnp.bfloat16

EPS = 1e-6
D_MODEL = 1024
D_FF = 4 * D_MODEL
CONV_CH = 512
CONV_WIDTH = 31
ATTN_WIDTH = 512
HEAD_DIM = 64
N_HEADS = 8
MOBA_BLOCK = 256
MOBA_TOP_K = 3
LRU_HEADS = 8
LRU_BLOCK = 128
LRU_CONV_WIDTH = 4
LRU_C = 8.0

LANES = 128
SUBLANES = 8
BF16_ROWS = 16
HEADS_PER_STEP = LANES // HEAD_DIM
MASKED = -1e30
VMEM_LIMIT = 56 * 1024 * 1024

ROW_TILE = 512
MLP_CHUNK = 1024
CONV_ROWS = 128


def _dot(a, b):
    return jnp.dot(a, b, preferred_element_type=F32)


def _dot_nt(a, b):
    return lax.dot_general(a, b, (((1,), (1,)), ((), ())), preferred_element_type=F32)


def _rmsnorm(x, g):
    return x * lax.rsqrt(jnp.mean(x * x, axis=-1, keepdims=True) + EPS) * g


def _const_spec(shape):
    zeros = (0,) * len(shape)
    return pl.BlockSpec(shape, lambda *_: zeros, pipeline_mode=pl.Buffered(1))


def _params(n_axes):
    return pltpu.CompilerParams(dimension_semantics=("arbitrary",) * n_axes,
                                vmem_limit_bytes=VMEM_LIMIT)


def _qkv_kernel(x_ref, g_ref, wk_ref, wqt_ref, wvt_ref, k_ref, qt_ref, vt_ref):
    h = _rmsnorm(x_ref[...], g_ref[...]).astype(BF16)
    k_ref[...] = _dot(h, wk_ref[...]).astype(BF16)
    qt_ref[...] = _dot_nt(wqt_ref[...], h).astype(BF16)
    vt_ref[...] = _dot_nt(wvt_ref[...], h).astype(BF16)


def _qkv(x, g, wk, wqt, wvt):
    batch, seq, _ = x.shape
    w_spec = _const_spec((D_MODEL, ATTN_WIDTH))
    wt_spec = _const_spec((ATTN_WIDTH, D_MODEL))
    t_spec = pl.BlockSpec((None, ATTN_WIDTH, ROW_TILE), lambda b, t: (b, 0, t))
    t_shape = jax.ShapeDtypeStruct((batch, ATTN_WIDTH, seq), BF16)
    return pl.pallas_call(
        _qkv_kernel,
        grid=(batch, seq // ROW_TILE),
        in_specs=[pl.BlockSpec((None, ROW_TILE, D_MODEL), lambda b, t: (b, t, 0)),
                  _const_spec((1, D_MODEL)), w_spec, wt_spec, wt_spec],
        out_specs=[pl.BlockSpec((None, ROW_TILE, ATTN_WIDTH), lambda b, t: (b, t, 0)), t_spec, t_spec],
        out_shape=[jax.ShapeDtypeStruct((batch, seq, ATTN_WIDTH), BF16), t_shape, t_shape],
        compiler_params=_params(2),
        name="qkv_proj",
    )(x, g, wk, wqt, wvt)


def _head_lanes(hh):
    qk0 = hh * HEAD_DIM
    ex0 = (1 - hh) * HEAD_DIM
    return qk0, ex0, ex0 + BF16_ROWS


V_ROWS = HEAD_DIM + BF16_ROWS


def _attn_prepare(k_ref, vt_ref, slope_ref, kaug_ref, kbar_ref, vta_ref):
    n_keys = k_ref.shape[0]
    k = k_ref[...]
    rowk = lax.broadcasted_iota(jnp.int32, (n_keys, LANES), 0)
    lane = lax.broadcasted_iota(jnp.int32, (n_keys, LANES), 1)
    tk = (rowk & (MOBA_BLOCK - 1)).astype(F32)
    blk = rowk >> (MOBA_BLOCK.bit_length() - 1)
    ones_rows = jnp.where(lax.broadcasted_iota(jnp.int32, (BF16_ROWS, n_keys), 0) == 0, 1.0, 0.0)
    for hh in range(HEADS_PER_STEP):
        qk0, ex0, bias0 = _head_lanes(hh)
        slope = slope_ref[hh:hh + 1, 0:LANES]
        extra = jnp.where(lane == ex0, 1.0,
                          jnp.where(lane == ex0 + 1, slope * tk,
                                    jnp.where(lane - bias0 == blk, 1.0, 0.0)))
        is_k = (lane >= qk0) & (lane < qk0 + HEAD_DIM)
        kaug_ref[hh] = jnp.where(is_k, k, extra.astype(BF16))
        vta_ref[hh, 0:HEAD_DIM, :] = vt_ref[qk0:qk0 + HEAD_DIM, :]
        vta_ref[hh, HEAD_DIM:V_ROWS, :] = ones_rows.astype(BF16)
    for j in range(n_keys // MOBA_BLOCK):
        kj = k_ref[j * MOBA_BLOCK:(j + 1) * MOBA_BLOCK, :].astype(F32)
        kbar_ref[j:j + 1, :] = jnp.sum(kj, axis=0, keepdims=True) * (1.0 / MOBA_BLOCK)


def _attn_block(c, qt_ref, slope_ref, kaug_ref, kbar_ref, vta_ref, o_ref):
    qt = qt_ref[...]
    rowi = lax.broadcasted_iota(jnp.int32, (SUBLANES, MOBA_BLOCK), 0)
    tq = lax.broadcasted_iota(jnp.int32, (SUBLANES, MOBA_BLOCK), 1).astype(F32)
    causal = (lax.broadcasted_iota(jnp.int32, (MOBA_BLOCK, MOBA_BLOCK), 0)
              <= lax.broadcasted_iota(jnp.int32, (MOBA_BLOCK, MOBA_BLOCK), 1))
    zeros8 = jnp.zeros((SUBLANES, MOBA_BLOCK), F32)
    outs = []
    for hh in range(HEADS_PER_STEP):
        qk0, ex0, bias0 = _head_lanes(hh)
        slope = slope_ref[hh:hh + 1, :]
        if c > 0:
            lane_k = lax.broadcasted_iota(jnp.int32, kbar_ref.shape, 1)
            kbar = jnp.where((lane_k >= qk0) & (lane_k < qk0 + HEAD_DIM), kbar_ref[...], 0.0)
            kb_hi = kbar.astype(BF16)
            kb_lo = (kbar - kb_hi.astype(F32)).astype(BF16)
            gate = _dot(kb_hi, qt) + _dot(kb_lo, qt)
            elig = rowi < c
            sel = elig & (jnp.abs(gate) < jnp.inf)
            if c > MOBA_TOP_K:
                beaten = zeros8
                for jp in range(c):
                    gp = gate[jp:jp + 1, :]
                    beats = (gp > gate) | ((gp == gate) & (rowi > jp))
                    beaten = beaten + jnp.where(beats, 1.0, 0.0)
                sel = sel & (beaten < float(MOBA_TOP_K))
            dist = (c - rowi).astype(F32) * float(MOBA_BLOCK)
            bias = jnp.where(elig, jnp.where(sel, -slope * dist, MASKED), 0.0)
        else:
            bias = zeros8
        ext_a = jnp.where(rowi == 0, -slope * tq, jnp.where(rowi == 1, 1.0, 0.0))
        ext = jnp.concatenate([ext_a, zeros8, bias, zeros8], axis=0).astype(BF16)
        pad = jnp.zeros((HEAD_DIM - ext.shape[0], MOBA_BLOCK), BF16)
        qs = qt[qk0:qk0 + HEAD_DIM, :] * jnp.asarray(HEAD_DIM ** -0.5, BF16)
        qat = jnp.concatenate([qs, ext, pad] if hh == 0 else [ext, pad, qs], axis=0)

        n_past = c * MOBA_BLOCK
        own = slice(n_past, n_past + MOBA_BLOCK)
        s_own = jnp.where(causal, _dot(kaug_ref[hh, own, :], qat), -jnp.inf)
        m = jnp.max(s_own, axis=0, keepdims=True)
        if c > 0:
            s_past = _dot(kaug_ref[hh, 0:n_past, :], qat)
            m = jnp.maximum(m, jnp.max(s_past, axis=0, keepdims=True))
        acc = _dot(vta_ref[hh, :, own], jnp.exp(s_own - m).astype(BF16))
        if c > 0:
            acc = acc + _dot(vta_ref[hh, :, 0:n_past], jnp.exp(s_past - m).astype(BF16))
        outs.append(acc[0:HEAD_DIM, :] * (1.0 / acc[HEAD_DIM:HEAD_DIM + 1, :]))
    o_ref[...] = jnp.concatenate(outs, axis=0).T.astype(o_ref.dtype)


def _attn_kernel(slope_ref, qt_ref, k_ref, vt_ref, o_ref, kaug_ref, kbar_ref, vta_ref):
    i = pl.program_id(2)

    @pl.when(i == 0)
    def _():
        _attn_prepare(k_ref, vt_ref, slope_ref, kaug_ref, kbar_ref, vta_ref)

    for c in range(k_ref.shape[0] // MOBA_BLOCK):
        @pl.when(i == c)
        def _(c=c):
            _attn_block(c, qt_ref, slope_ref, kaug_ref, kbar_ref, vta_ref, o_ref)


def _attention(qt, k, vt):
    batch, seq, _ = k.shape
    n_blk = seq // MOBA_BLOCK
    pairs = N_HEADS // HEADS_PER_STEP
    slopes = 2.0 ** (-8.0 * jnp.arange(1, N_HEADS + 1, dtype=F32) / N_HEADS)
    slopes = jnp.broadcast_to(slopes.reshape(pairs, HEADS_PER_STEP, 1), (pairs, HEADS_PER_STEP, MOBA_BLOCK))
    slopes = jnp.pad(slopes, ((0, 0), (0, SUBLANES - HEADS_PER_STEP), (0, 0)))
    return pl.pallas_call(
        _attn_kernel,
        grid=(batch, pairs, n_blk),
        in_specs=[
            pl.BlockSpec((None, SUBLANES, MOBA_BLOCK), lambda b, hp, i: (hp, 0, 0)),
            pl.BlockSpec((None, LANES, MOBA_BLOCK), lambda b, hp, i: (b, hp, i)),
            pl.BlockSpec((None, seq, LANES), lambda b, hp, i: (b, 0, hp)),
            pl.BlockSpec((None, LANES, seq), lambda b, hp, i: (b, hp, 0)),
        ],
        out_specs=pl.BlockSpec((None, MOBA_BLOCK, LANES), lambda b, hp, i: (b, i, hp)),
        out_shape=jax.ShapeDtypeStruct(k.shape, BF16),
        scratch_shapes=[pltpu.VMEM((HEADS_PER_STEP, seq, LANES), BF16),
                        pltpu.VMEM((n_blk, LANES), F32),
                        pltpu.VMEM((HEADS_PER_STEP, V_ROWS, seq), BF16)],
        compiler_params=_params(3),
        name="moba_attention",
    )(slopes, qt, k, vt)


def _to_time_major(val, buf_ref, row0, batch, t_blk):
    for s in range(val.shape[1] // LANES):
        for b in range(batch):
            buf_ref[s, pl.ds(row0 + b, t_blk, stride=batch), :] = (
                val[b * t_blk:(b + 1) * t_blk, s * LANES:(s + 1) * LANES])


def _from_time_major(buf_ref, s, b, batch, t_blk):
    return buf_ref[s, pl.ds(b, t_blk, stride=batch), :]


def _mix0_kernel(x_ref, yb_ref, g_ref, wz_ref, cw_ref, lg_ref, lb_ref, w_ref, o_ref,
                 zbuf_ref, conv_ref, lhs_ref):
    i = pl.program_id(0)
    batch, t_blk, _ = x_ref.shape
    rows = batch * t_blk
    n_slab = CONV_CH // LANES

    @pl.when(i == 0)
    def _():
        zbuf_ref[:, 0:rows, :] = jnp.zeros((n_slab, rows, LANES), F32)

    @pl.when(i > 0)
    def _():
        zbuf_ref[:, 0:rows, :] = zbuf_ref[:, rows:2 * rows, :]

    x = x_ref[...].reshape(rows, D_MODEL)
    h = _rmsnorm(x, g_ref[...]).astype(BF16)
    z = _dot(h, wz_ref[:, 0:CONV_CH]) * jax.nn.sigmoid(_dot(h, wz_ref[:, CONV_CH:2 * CONV_CH]))
    _to_time_major(z, zbuf_ref, rows, batch, t_blk)

    for s in range(n_slab):
        def conv_rows(rb, carry, s=s):
            r0 = rb * CONV_ROWS
            acc = jnp.zeros((CONV_ROWS, LANES), F32)
            for j in range(CONV_WIDTH):
                start = pl.multiple_of(r0 + (rows - (CONV_WIDTH - 1 - j) * batch), batch)
                acc = acc + zbuf_ref[s, pl.ds(start, CONV_ROWS), :] * cw_ref[j:j + 1, s * LANES:(s + 1) * LANES]
            conv_ref[s, pl.ds(pl.multiple_of(r0, CONV_ROWS), CONV_ROWS), :] = acc
            return carry

        lax.fori_loop(0, rows // CONV_ROWS, conv_rows, 0)

    y = [conv_ref[s] for s in range(n_slab)]
    mu = jnp.sum(y[0] + y[1] + y[2] + y[3], axis=-1, keepdims=True) * (1.0 / CONV_CH)
    d = [ys - mu for ys in y]
    var = jnp.sum(d[0] * d[0] + d[1] * d[1] + d[2] * d[2] + d[3] * d[3], axis=-1, keepdims=True) * (1.0 / CONV_CH)
    inv = lax.rsqrt(var + EPS)
    for s in range(n_slab):
        cols = slice(s * LANES, (s + 1) * LANES)
        yn = d[s] * inv * lg_ref[:, cols] + lb_ref[:, cols]
        conv_ref[s] = yn * jax.nn.sigmoid(yn)
    for b in range(batch):
        for s in range(n_slab):
            lhs_ref[b * t_blk:(b + 1) * t_blk, s * LANES:(s + 1) * LANES] = (
                _from_time_major(conv_ref, s, b, batch, t_blk).astype(BF16))

    out = (x + _dot(lhs_ref[...], w_ref[0:CONV_CH, :])
           + _dot(yb_ref[...].reshape(rows, ATTN_WIDTH), w_ref[CONV_CH:CONV_CH + ATTN_WIDTH, :]))
    o_ref[...] = out.reshape(batch, t_blk, D_MODEL)


def _mix0(x, yb, g, w_z, conv_w, ln_g, ln_b, w_out):
    batch, seq, _ = x.shape
    t_blk = ROW_TILE // batch
    assert (CONV_WIDTH - 1) * batch <= ROW_TILE and CONV_CH == 4 * LANES
    blk = lambda width: pl.BlockSpec((batch, t_blk, width), lambda i: (0, i, 0))
    vec = _const_spec((1, CONV_CH))
    return pl.pallas_call(
        _mix0_kernel,
        grid=(seq // t_blk,),
        in_specs=[blk(D_MODEL), blk(ATTN_WIDTH), _const_spec((1, D_MODEL)), _const_spec(w_z.shape),
                  _const_spec(conv_w.shape), vec, vec, _const_spec(w_out.shape)],
        out_specs=blk(D_MODEL),
        out_shape=jax.ShapeDtypeStruct(x.shape, F32),
        scratch_shapes=[pltpu.VMEM((CONV_CH // LANES, 2 * ROW_TILE, LANES), F32),
                        pltpu.VMEM((CONV_CH // LANES, ROW_TILE, LANES), F32),
                        pltpu.VMEM((ROW_TILE, CONV_CH), BF16)],
        compiler_params=_params(1),
        name="mixer0",
    )(x, yb, g, w_z, conv_w, ln_g, ln_b, w_out)


def _mlp_kernel(*refs, final):
    if final:
        x_ref, g_ref, wu_ref, wd_ref, fg_ref, o_ref = refs
    else:
        x_ref, g_ref, wu_ref, wd_ref, o_ref = refs
    x = x_ref[...]
    h = _rmsnorm(x, g_ref[...]).astype(BF16)
    acc = x
    for c in range(D_FF // MLP_CHUNK):
        a = _dot(h, wu_ref[:, c * MLP_CHUNK:(c + 1) * MLP_CHUNK])
        a = jnp.square(jnp.maximum(a, 0.0)).astype(BF16)
        acc = acc + _dot(a, wd_ref[c * MLP_CHUNK:(c + 1) * MLP_CHUNK, :])
    if final:
        acc = _rmsnorm(acc, fg_ref[...])
    o_ref[...] = acc


def _mlp(x, g, w_up, w_down, final_g=None):
    rows = x.shape[0]
    final = final_g is not None
    row = pl.BlockSpec((ROW_TILE, D_MODEL), lambda i: (i, 0))
    in_specs = [row, _const_spec((1, D_MODEL)), _const_spec(w_up.shape), _const_spec(w_down.shape)]
    args = [x, g, w_up, w_down]
    if final:
        in_specs.append(_const_spec((1, D_MODEL)))
        args.append(final_g)
    return pl.pallas_call(
        functools.partial(_mlp_kernel, final=final),
        grid=(rows // ROW_TILE,),
        in_specs=in_specs,
        out_specs=row,
        out_shape=jax.ShapeDtypeStruct((rows, D_MODEL), F32),
        compiler_params=_params(1),
        name="mlp_final" if final else "mlp",
    )(*args)


def _gelu_tanh(x):
    return 0.5 * x * (1.0 + jnp.tanh(0.7978845608028654 * (x + 0.044715 * (x * x * x))))


def _lru_kernel(x_ref, g_ref, win_ref, cw_ref, cb_ref, wri_ref, br_ref, bi_ref, lam_ref, wout_ref,
                o_ref, xbuf_ref, a_ref, b_ref, gate_ref, h_ref, y_ref):
    i = pl.program_id(0)
    batch, t_blk, _ = x_ref.shape
    rows = batch * t_blk
    halo = xbuf_ref.shape[1] - rows

    @pl.when(i == 0)
    def _():
        xbuf_ref[:, 0:halo, :] = jnp.zeros((LRU_HEADS, halo, LANES), F32)
        h_ref[...] = jnp.zeros(h_ref.shape, F32)

    @pl.when(i > 0)
    def _():
        xbuf_ref[:, 0:halo, :] = xbuf_ref[:, rows:rows + halo, :]

    x = x_ref[...].reshape(rows, D_MODEL)
    h = _rmsnorm(x, g_ref[...]).astype(BF16)
    gate_ref[...] = _gelu_tanh(_dot(h, win_ref[:, 0:D_MODEL]))
    _to_time_major(_dot(h, win_ref[:, D_MODEL:2 * D_MODEL]), xbuf_ref, halo, batch, t_blk)

    neg_lam = -lam_ref[...]
    softplus = jnp.maximum(neg_lam, 0.0) + jnp.log1p(jnp.exp(-jnp.abs(neg_lam)))
    for hd in range(LRU_HEADS):
        cols = slice(hd * LRU_BLOCK, (hd + 1) * LRU_BLOCK)
        xc = cb_ref[:, cols]
        for j in range(LRU_CONV_WIDTH):
            start = halo - (LRU_CONV_WIDTH - 1 - j) * batch
            xc = xc + xbuf_ref[hd, start:start + rows, :] * cw_ref[j:j + 1, cols]
        ri = _dot(xc.astype(BF16), wri_ref[hd])
        r = jax.nn.sigmoid(ri[:, 0:LRU_BLOCK] + br_ref[:, cols])
        ig = jax.nn.sigmoid(ri[:, LRU_BLOCK:2 * LRU_BLOCK] + bi_ref[:, cols])
        log_a = -LRU_C * r * softplus[:, cols]
        a = jnp.exp(log_a)
        a_ref[hd] = a
        b_ref[hd] = jnp.sqrt(-jnp.tanh(log_a) * (a * a + 1.0)) * (ig * xc)

    state = [h_ref[hd] for hd in range(LRU_HEADS)]
    for t in range(t_blk):
        step = slice(t * batch, (t + 1) * batch)
        for hd in range(LRU_HEADS):
            state[hd] = a_ref[hd, step, :] * state[hd] + b_ref[hd, step, :]
            b_ref[hd, step, :] = state[hd]
    for hd in range(LRU_HEADS):
        h_ref[hd] = state[hd]

    for b in range(batch):
        blk = slice(b * t_blk, (b + 1) * t_blk)
        for hd in range(LRU_HEADS):
            cols = slice(hd * LRU_BLOCK, (hd + 1) * LRU_BLOCK)
            y_ref[blk, cols] = (_from_time_major(b_ref, hd, b, batch, t_blk) * gate_ref[blk, cols]).astype(BF16)
    o_ref[...] = (x + _dot(y_ref[...], wout_ref[...])).reshape(batch, t_blk, D_MODEL)


def _lru(x, g, w_in, conv_w, conv_b, w_ri, b_r, b_i, lam, w_out):
    batch, seq, _ = x.shape
    t_blk = ROW_TILE // batch
    halo = 64
    assert (LRU_CONV_WIDTH - 1) * batch <= halo and LRU_BLOCK == LANES
    blk = pl.BlockSpec((batch, t_blk, D_MODEL), lambda i: (0, i, 0))
    vec = _const_spec((1, D_MODEL))
    slab = lambda n: pltpu.VMEM((LRU_HEADS, n, LANES), F32)
    return pl.pallas_call(
        _lru_kernel,
        grid=(seq // t_blk,),
        in_specs=[blk, vec, _const_spec(w_in.shape), _const_spec(conv_w.shape), vec,
                  _const_spec(w_ri.shape), vec, vec, vec, _const_spec(w_out.shape)],
        out_specs=blk,
        out_shape=jax.ShapeDtypeStruct(x.shape, F32),
        scratch_shapes=[slab(halo + ROW_TILE), slab(ROW_TILE), slab(ROW_TILE),
                        pltpu.VMEM((ROW_TILE, D_MODEL), F32), slab(batch),
                        pltpu.VMEM((ROW_TILE, D_MODEL), BF16)],
        compiler_params=_params(1),
        name="rglru_mixer",
    )(x, g, w_in, conv_w, conv_b, w_ri, b_r, b_i, lam, w_out)


def kernel(x, mix_norm, mlp_norm, w_up, w_down, ab_w_in, ab_conv_w, ab_ln_g, ab_ln_b, ab_w_out,
           c_w_in, c_conv_w, c_conv_b, c_w_r, c_b_r, c_w_i, c_b_i, c_lambda, c_w_out, final_norm):
    batch, seq, d = x.shape
    vec = lambda a: a.reshape(1, -1).astype(F32)
    flat = lambda a: a.reshape(batch * seq, d)

    w_in = ab_w_in[0]
    o = 2 * CONV_CH
    w_q, w_k, w_v = (w_in[:, o + n * ATTN_WIDTH:o + (n + 1) * ATTN_WIDTH] for n in range(3))
    k, qt, vt = _qkv(x, vec(mix_norm[0]), w_k.astype(BF16), w_q.T.astype(BF16), w_v.T.astype(BF16))
    yb = _attention(qt, k, vt)
    conv_w = jnp.pad(ab_conv_w[0], ((0, 32 - CONV_WIDTH), (0, 0)))
    x = _mix0(x, yb, vec(mix_norm[0]), w_in[:, 0:o].astype(BF16), conv_w, vec(ab_ln_g[0]), vec(ab_ln_b[0]),
              ab_w_out[0].astype(BF16))
    x = _mlp(flat(x), vec(mlp_norm[0]), w_up[0].astype(BF16), w_down[0].astype(BF16)).reshape(batch, seq, d)

    w_ri = jnp.concatenate([c_w_r[0], c_w_i[0]], axis=-1).astype(BF16)
    lru_conv_w = jnp.pad(c_conv_w[0], ((0, SUBLANES - LRU_CONV_WIDTH), (0, 0)))
    x = _lru(x, vec(mix_norm[1]), c_w_in[0].astype(BF16), lru_conv_w, vec(c_conv_b[0]), w_ri,
             vec(c_b_r[0]), vec(c_b_i[0]), vec(c_lambda[0]), c_w_out[0].astype(BF16))
    x = _mlp(flat(x), vec(mlp_norm[1]), w_up[1].astype(BF16), w_down[1].astype(BF16), vec(final_norm))
    return x.reshape(batch, seq, d)
```

```python
import functools

import jax
import jax.numpy as jnp
from jax import lax
from jax.experimental import pallas as pl
from jax.experimental.pallas import tpu as pltpu

F32 = jnp.float32
BF16 = jnp.bfloat16

EPS = 1e-6
D_MODEL = 1024
D_FF = 4 * D_MODEL
CONV_CH = 512
CONV_WIDTH = 31
ATTN_WIDTH = 512
HEAD_DIM = 64
N_HEADS = 8
MOBA_BLOCK = 256
MOBA_TOP_K = 3
LRU_HEADS = 8
LRU_BLOCK = 128
LRU_CONV_WIDTH = 4
LRU_C = 8.0

LANES = 128
SUBLANES = 8
BF16_ROWS = 16
HEADS_PER_STEP = LANES // HEAD_DIM
MASKED = -1e30
VMEM_LIMIT = 56 * 1024 * 1024

ROW_TILE = 512
MLP_CHUNK = 1024
CONV_ROWS = 128


def _dot(a, b):
    return jnp.dot(a, b, preferred_element_type=F32)


def _dot_nt(a, b):
    return lax.dot_general(a, b, (((1,), (1,)), ((), ())), preferred_element_type=F32)


def _rmsnorm(x, g):
    return x * lax.rsqrt(jnp.mean(x * x, axis=-1, keepdims=True) + EPS) * g


def _const_spec(shape):
    zeros = (0,) * len(shape)
    return pl.BlockSpec(shape, lambda *_: zeros, pipeline_mode=pl.Buffered(1))


def _params(n_axes):
    return pltpu.CompilerParams(dimension_semantics=("arbitrary",) * n_axes,
                                vmem_limit_bytes=VMEM_LIMIT)


def _qkv_kernel(x_ref, g_ref, wk_ref, wqt_ref, wvt_ref, k_ref, qt_ref, vt_ref):
    h = _rmsnorm(x_ref[...], g_ref[...]).astype(BF16)
    k_ref[...] = _dot(h, wk_ref[...]).astype(BF16)
    qt_ref[...] = _dot_nt(wqt_ref[...], h).astype(BF16)
    vt_ref[...] = _dot_nt(wvt_ref[...], h).astype(BF16)


def _qkv(x, g, wk, wqt, wvt):
    batch, seq, _ = x.shape
    w_spec = _const_spec((D_MODEL, ATTN_WIDTH))
    wt_spec = _const_spec((ATTN_WIDTH, D_MODEL))
    t_spec = pl.BlockSpec((None, ATTN_WIDTH, ROW_TILE), lambda b, t: (b, 0, t))
    t_shape = jax.ShapeDtypeStruct((batch, ATTN_WIDTH, seq), BF16)
    return pl.pallas_call(
        _qkv_kernel,
        grid=(batch, seq // ROW_TILE),
        in_specs=[pl.BlockSpec((None, ROW_TILE, D_MODEL), lambda b, t: (b, t, 0)),
                  _const_spec((1, D_MODEL)), w_spec, wt_spec, wt_spec],
        out_specs=[pl.BlockSpec((None, ROW_TILE, ATTN_WIDTH), lambda b, t: (b, t, 0)), t_spec, t_spec],
        out_shape=[jax.ShapeDtypeStruct((batch, seq, ATTN_WIDTH), BF16), t_shape, t_shape],
        compiler_params=_params(2),
        name="qkv_proj",
    )(x, g, wk, wqt, wvt)


def _head_lanes(hh):
    qk0 = hh * HEAD_DIM
    ex0 = (1 - hh) * HEAD_DIM
    return qk0, ex0, ex0 + BF16_ROWS


V_ROWS = HEAD_DIM + BF16_ROWS


def _attn_prepare(k_ref, vt_ref, slope_ref, kaug_ref, kbar_ref, vta_ref):
    n_keys = k_ref.shape[0]
    rowk = lax.broadcasted_iota(jnp.int32, (n_keys, LANES), 0)
    lane = lax.broadcasted_iota(jnp.int32, (n_keys, LANES), 1)
    tk = (rowk & (MOBA_BLOCK - 1)).astype(F32)
    blk = rowk >> (MOBA_BLOCK.bit_length() - 1)
    ones_rows = jnp.where(lax.broadcasted_iota(jnp.int32, (BF16_ROWS, n_keys), 0) == 0, 1.0, 0.0)
    for h in range(N_HEADS):
        pair, hh = divmod(h, HEADS_PER_STEP)
        qk0, ex0, bias0 = _head_lanes(hh)
        slope = slope_ref[h:h + 1, 0:LANES]
        extra = jnp.where(lane == ex0, 1.0,
                          jnp.where(lane == ex0 + 1, slope * tk,
                                    jnp.where(lane - bias0 == blk, 1.0, 0.0)))
        is_k = (lane >= qk0) & (lane < qk0 + HEAD_DIM)
        kaug_ref[h] = jnp.where(is_k, k_ref[:, pair * LANES:(pair + 1) * LANES], extra.astype(BF16))
        vta_ref[h, 0:HEAD_DIM, :] = vt_ref[h * HEAD_DIM:(h + 1) * HEAD_DIM, :]
        vta_ref[h, HEAD_DIM:V_ROWS, :] = ones_rows.astype(BF16)
    for j in range(n_keys // MOBA_BLOCK):
        kj = k_ref[j * MOBA_BLOCK:(j + 1) * MOBA_BLOCK, :].astype(F32)
        kbar_ref[j:j + 1, :] = jnp.sum(kj, axis=0, keepdims=True) * (1.0 / MOBA_BLOCK)


def _attn_block(c, qt_ref, slope_ref, kaug_ref, kbar_ref, vta_ref, o_ref):
    rowi = lax.broadcasted_iota(jnp.int32, (SUBLANES, MOBA_BLOCK), 0)
    tq = lax.broadcasted_iota(jnp.int32, (SUBLANES, MOBA_BLOCK), 1).astype(F32)
    causal = (lax.broadcasted_iota(jnp.int32, (MOBA_BLOCK, MOBA_BLOCK), 0)
              <= lax.broadcasted_iota(jnp.int32, (MOBA_BLOCK, MOBA_BLOCK), 1))
    zeros8 = jnp.zeros((SUBLANES, MOBA_BLOCK), F32)
    lane_k = lax.broadcasted_iota(jnp.int32, (SUBLANES, LANES), 1)
    n_past = c * MOBA_BLOCK
    own = slice(n_past, n_past + MOBA_BLOCK)

    qats = []
    for pair in range(N_HEADS // HEADS_PER_STEP):
        qt = qt_ref[pair * LANES:(pair + 1) * LANES, :]
        if c > 0:
            kbar = kbar_ref[:, pair * LANES:(pair + 1) * LANES]
            kbar = jnp.concatenate([jnp.where(lane_k < HEAD_DIM, kbar, 0.0),
                                    jnp.where(lane_k >= HEAD_DIM, kbar, 0.0)], axis=0)
            kb_hi = kbar.astype(BF16)
            kb_lo = (kbar - kb_hi.astype(F32)).astype(BF16)
            gates = _dot(kb_hi, qt) + _dot(kb_lo, qt)
        for hh in range(HEADS_PER_STEP):
            h = pair * HEADS_PER_STEP + hh
            qk0, ex0, bias0 = _head_lanes(hh)
            slope = slope_ref[h:h + 1, :]
            if c > 0:
                gate = gates[hh * SUBLANES:(hh + 1) * SUBLANES, :]
                elig = rowi < c
                sel = elig & (jnp.abs(gate) < jnp.inf)
                if c > MOBA_TOP_K:
                    beaten = zeros8
                    for jp in range(c):
                        gp = gate[jp:jp + 1, :]
                        beats = (gp > gate) | ((gp == gate) & (rowi > jp))
                        beaten = beaten + jnp.where(beats, 1.0, 0.0)
                    sel = sel & (beaten < float(MOBA_TOP_K))
                dist = (c - rowi).astype(F32) * float(MOBA_BLOCK)
                bias = jnp.where(elig, jnp.where(sel, -slope * dist, MASKED), 0.0)
            else:
                bias = zeros8
            ext_a = jnp.where(rowi == 0, -slope * tq, jnp.where(rowi == 1, 1.0, 0.0))
            ext = jnp.concatenate([ext_a, zeros8, bias, zeros8], axis=0).astype(BF16)
            pad = jnp.zeros((HEAD_DIM - ext.shape[0], MOBA_BLOCK), BF16)
            qs = qt[qk0:qk0 + HEAD_DIM, :] * jnp.asarray(HEAD_DIM ** -0.5, BF16)
            qats.append(jnp.concatenate([qs, ext, pad] if hh == 0 else [ext, pad, qs], axis=0))

    def logits(h):
        s_own = jnp.where(causal, _dot(kaug_ref[h, own, :], qats[h]), -jnp.inf)
        m = jnp.max(s_own, axis=0, keepdims=True)
        s_past = None
        if c > 0:
            s_past = _dot(kaug_ref[h, 0:n_past, :], qats[h])
            m = jnp.maximum(m, jnp.max(s_past, axis=0, keepdims=True))
        return s_own, s_past, m

    def weighted_values(h, s_own, s_past, m):
        acc = _dot(vta_ref[h, :, own], jnp.exp(s_own - m).astype(BF16))
        if c > 0:
            acc = acc + _dot(vta_ref[h, :, 0:n_past], jnp.exp(s_past - m).astype(BF16))
        return acc[0:HEAD_DIM, :] * (1.0 / acc[HEAD_DIM:HEAD_DIM + 1, :])

    outs = []
    state = logits(0)
    for h in range(N_HEADS):
        nxt = logits(h + 1) if h + 1 < N_HEADS else None
        outs.append(weighted_values(h, *state))
        state = nxt
    for pair in range(N_HEADS // HEADS_PER_STEP):
        o_pair = jnp.concatenate(outs[pair * HEADS_PER_STEP:(pair + 1) * HEADS_PER_STEP], axis=0)
        o_ref[:, pair * LANES:(pair + 1) * LANES] = o_pair.T.astype(o_ref.dtype)


def _attn_kernel(slope_ref, qt_ref, k_ref, vt_ref, o_ref, kaug_ref, kbar_ref, vta_ref):
    i = pl.program_id(1)

    @pl.when(i == 0)
    def _():
        _attn_prepare(k_ref, vt_ref, slope_ref, kaug_ref, kbar_ref, vta_ref)

    for c in range(k_ref.shape[0] // MOBA_BLOCK):
        @pl.when(i == c)
        def _(c=c):
            _attn_block(c, qt_ref, slope_ref, kaug_ref, kbar_ref, vta_ref, o_ref)


def _attention(qt, k, vt):
    batch, seq, _ = k.shape
    n_blk = seq // MOBA_BLOCK
    slopes = 2.0 ** (-8.0 * jnp.arange(1, N_HEADS + 1, dtype=F32) / N_HEADS)
    slopes = jnp.broadcast_to(slopes.reshape(N_HEADS, 1), (N_HEADS, MOBA_BLOCK))
    return pl.pallas_call(
        _attn_kernel,
        grid=(batch, n_blk),
        in_specs=[
            _const_spec((N_HEADS, MOBA_BLOCK)),
            pl.BlockSpec((None, ATTN_WIDTH, MOBA_BLOCK), lambda b, i: (b, 0, i)),
            pl.BlockSpec((None, seq, ATTN_WIDTH), lambda b, i: (b, 0, 0)),
            pl.BlockSpec((None, ATTN_WIDTH, seq), lambda b, i: (b, 0, 0)),
        ],
        out_specs=pl.BlockSpec((None, MOBA_BLOCK, ATTN_WIDTH), lambda b, i: (b, i, 0)),
        out_shape=jax.ShapeDtypeStruct(k.shape, BF16),
        scratch_shapes=[pltpu.VMEM((N_HEADS, seq, LANES), BF16),
                        pltpu.VMEM((n_blk, ATTN_WIDTH), F32),
                        pltpu.VMEM((N_HEADS, V_ROWS, seq), BF16)],
        compiler_params=_params(2),
        name="moba_attention",
    )(slopes, qt, k, vt)


def _to_time_major(val, buf_ref, slab0, row0, batch, t_blk):
    for s in range(val.shape[1] // LANES):
        for b in range(batch):
            buf_ref[slab0 + s, pl.ds(row0 + b, t_blk, stride=batch), :] = (
                val[b * t_blk:(b + 1) * t_blk, s * LANES:(s + 1) * LANES])


def _from_time_major(buf_ref, s, b, batch, t_blk):
    return buf_ref[s, pl.ds(b, t_blk, stride=batch), :]


def _mix0_kernel(x_ref, yb_ref, g_ref, wz_ref, cw_ref, lg_ref, lb_ref, w_ref, o_ref,
                 zbuf_ref, conv_ref, lhs_ref):
    i = pl.program_id(0)
    batch, t_blk, _ = x_ref.shape
    rows = batch * t_blk
    n_slab = CONV_CH // LANES

    @pl.when(i == 0)
    def _():
        zbuf_ref[:, 0:rows, :] = jnp.zeros((n_slab, rows, LANES), F32)

    @pl.when(i > 0)
    def _():
        zbuf_ref[:, 0:rows, :] = zbuf_ref[:, rows:2 * rows, :]

    x = x_ref[...].reshape(rows, D_MODEL)
    h = _rmsnorm(x, g_ref[...]).astype(BF16)

    def project(s):
        return _dot(h, wz_ref[:, 2 * s * LANES:2 * (s + 1) * LANES])

    def conv(s, u):
        z = u[:, 0:LANES] * jax.nn.sigmoid(u[:, LANES:2 * LANES])
        _to_time_major(z, zbuf_ref, s, rows, batch, t_blk)
        for r0 in range(0, rows, CONV_ROWS):
            acc = jnp.zeros((CONV_ROWS, LANES), F32)
            for j in range(CONV_WIDTH):
                start = r0 + rows - (CONV_WIDTH - 1 - j) * batch
                acc = acc + zbuf_ref[s, start:start + CONV_ROWS, :] * cw_ref[j:j + 1, s * LANES:(s + 1) * LANES]
            conv_ref[s, r0:r0 + CONV_ROWS, :] = acc

    def normalize(s, mu, inv):
        cols = slice(s * LANES, (s + 1) * LANES)
        yn = (conv_ref[s] - mu) * inv * lg_ref[:, cols] + lb_ref[:, cols]
        conv_ref[s] = yn * jax.nn.sigmoid(yn)
        for b in range(batch):
            lhs_ref[b * t_blk:(b + 1) * t_blk, cols] = _from_time_major(conv_ref, s, b, batch, t_blk).astype(BF16)

    u = project(0)
    for s in range(n_slab):
        u_next = project(s + 1) if s + 1 < n_slab else None
        if s == n_slab - 1:
            out = x + _dot(yb_ref[...].reshape(rows, ATTN_WIDTH), w_ref[CONV_CH:CONV_CH + ATTN_WIDTH, :])
        conv(s, u)
        u = u_next

    y = [conv_ref[s] for s in range(n_slab)]
    mu = jnp.sum(y[0] + y[1] + y[2] + y[3], axis=-1, keepdims=True) * (1.0 / CONV_CH)
    d = [ys - mu for ys in y]
    var = jnp.sum(d[0] * d[0] + d[1] * d[1] + d[2] * d[2] + d[3] * d[3], axis=-1, keepdims=True) * (1.0 / CONV_CH)
    inv = lax.rsqrt(var + EPS)
    half = n_slab // 2
    for first in (0, half):
        for s in range(first, first + half):
            normalize(s, mu, inv)
        k_rows = slice(first * LANES, (first + half) * LANES)
        out = out + _dot(lhs_ref[:, k_rows], w_ref[k_rows, :])
    o_ref[...] = out.reshape(batch, t_blk, D_MODEL)


def _mix0(x, yb, g, w_z, conv_w, ln_g, ln_b, w_out):
    batch, seq, _ = x.shape
    t_blk = ROW_TILE // batch
    assert (CONV_WIDTH - 1) * batch <= ROW_TILE and CONV_CH == 4 * LANES
    blk = lambda width: pl.BlockSpec((batch, t_blk, width), lambda i: (0, i, 0))
    vec = _const_spec((1, CONV_CH))
    return pl.pallas_call(
        _mix0_kernel,
        grid=(seq // t_blk,),
        in_specs=[blk(D_MODEL), blk(ATTN_WIDTH), _const_spec((1, D_MODEL)), _const_spec(w_z.shape),
                  _const_spec(conv_w.shape), vec, vec, _const_spec(w_out.shape)],
        out_specs=blk(D_MODEL),
        out_shape=jax.ShapeDtypeStruct(x.shape, F32),
        scratch_shapes=[pltpu.VMEM((CONV_CH // LANES, 2 * ROW_TILE, LANES), F32),
                        pltpu.VMEM((CONV_CH // LANES, ROW_TILE, LANES), F32),
                        pltpu.VMEM((ROW_TILE, CONV_CH), BF16)],
        compiler_params=_params(1),
        name="mixer0",
    )(x, yb, g, w_z, conv_w, ln_g, ln_b, w_out)


def _mlp_kernel(*refs, final):
    if final:
        x_ref, g_ref, wu_ref, wd_ref, fg_ref, o_ref = refs
    else:
        x_ref, g_ref, wu_ref, wd_ref, o_ref = refs
    x = x_ref[...]
    h = _rmsnorm(x, g_ref[...]).astype(BF16)
    acc = x
    for c in range(D_FF // MLP_CHUNK):
        a = _dot(h, wu_ref[:, c * MLP_CHUNK:(c + 1) * MLP_CHUNK])
        a = jnp.square(jnp.maximum(a, 0.0)).astype(BF16)
        acc = acc + _dot(a, wd_ref[c * MLP_CHUNK:(c + 1) * MLP_CHUNK, :])
    if final:
        acc = _rmsnorm(acc, fg_ref[...])
    o_ref[...] = acc


def _mlp(x, g, w_up, w_down, final_g=None):
    rows = x.shape[0]
    final = final_g is not None
    row = pl.BlockSpec((ROW_TILE, D_MODEL), lambda i: (i, 0))
    in_specs = [row, _const_spec((1, D_MODEL)), _const_spec(w_up.shape), _const_spec(w_down.shape)]
    args = [x, g, w_up, w_down]
    if final:
        in_specs.append(_const_spec((1, D_MODEL)))
        args.append(final_g)
    return pl.pallas_call(
        functools.partial(_mlp_kernel, final=final),
        grid=(rows // ROW_TILE,),
        in_specs=in_specs,
        out_specs=row,
        out_shape=jax.ShapeDtypeStruct((rows, D_MODEL), F32),
        compiler_params=_params(1),
        name="mlp_final" if final else "mlp",
    )(*args)


def _gelu_tanh(x):
    return 0.5 * x * (1.0 + jnp.tanh(0.7978845608028654 * (x + 0.044715 * (x * x * x))))


def _lru_kernel(x_ref, g_ref, win_ref, cw_ref, cb_ref, wri_ref, br_ref, bi_ref, lam_ref, wout_ref,
                o_ref, xbuf_ref, a_ref, b_ref, gate_ref, h_ref, y_ref):
    i = pl.program_id(0)
    batch, t_blk, _ = x_ref.shape
    rows = batch * t_blk
    halo = xbuf_ref.shape[1] - rows

    @pl.when(i == 0)
    def _():
        xbuf_ref[:, 0:halo, :] = jnp.zeros((LRU_HEADS, halo, LANES), F32)
        h_ref[...] = jnp.zeros(h_ref.shape, F32)

    @pl.when(i > 0)
    def _():
        xbuf_ref[:, 0:halo, :] = xbuf_ref[:, rows:rows + halo, :]

    x = x_ref[...].reshape(rows, D_MODEL)
    h = _rmsnorm(x, g_ref[...]).astype(BF16)
    neg_lam = -lam_ref[...]
    softplus = jnp.maximum(neg_lam, 0.0) + jnp.log1p(jnp.exp(-jnp.abs(neg_lam)))
    group = 2
    width = group * LRU_BLOCK

    def project(p):
        cols = slice(p * width, (p + 1) * width)
        gate_ref[:, cols] = _gelu_tanh(_dot(h, win_ref[:, cols]))
        xr = _dot(h, win_ref[:, D_MODEL + p * width:D_MODEL + (p + 1) * width])
        _to_time_major(xr, xbuf_ref, p * group, halo, batch, t_blk)

    def recurrence_inputs(hd):
        cols = slice(hd * LRU_BLOCK, (hd + 1) * LRU_BLOCK)
        xc = cb_ref[:, cols]
        for j in range(LRU_CONV_WIDTH):
            start = halo - (LRU_CONV_WIDTH - 1 - j) * batch
            xc = xc + xbuf_ref[hd, start:start + rows, :] * cw_ref[j:j + 1, cols]
        ri = _dot(xc.astype(BF16), wri_ref[hd])
        r = jax.nn.sigmoid(ri[:, 0:LRU_BLOCK] + br_ref[:, cols])
        ig = jax.nn.sigmoid(ri[:, LRU_BLOCK:2 * LRU_BLOCK] + bi_ref[:, cols])
        log_a = -LRU_C * r * softplus[:, cols]
        a = jnp.exp(log_a)
        a_ref[hd] = a
        b_ref[hd] = jnp.sqrt(-jnp.tanh(log_a) * (a * a + 1.0)) * (ig * xc)

    def scan(p):
        heads = range(p * group, (p + 1) * group)
        state = {hd: h_ref[hd] for hd in heads}
        for t in range(t_blk):
            step = slice(t * batch, (t + 1) * batch)
            for hd in heads:
                state[hd] = a_ref[hd, step, :] * state[hd] + b_ref[hd, step, :]
                b_ref[hd, step, :] = state[hd]
        for hd in heads:
            h_ref[hd] = state[hd]
            cols = slice(hd * LRU_BLOCK, (hd + 1) * LRU_BLOCK)
            for b in range(batch):
                blk = slice(b * t_blk, (b + 1) * t_blk)
                y_ref[blk, cols] = (_from_time_major(b_ref, hd, b, batch, t_blk) * gate_ref[blk, cols]).astype(BF16)

    def out_part(p):
        k_rows = slice(p * width, (p + 1) * width)
        return _dot(y_ref[:, k_rows], wout_ref[k_rows, :])

    n_groups = LRU_HEADS // group
    out = x
    project(0)
    for p in range(n_groups):
        if p + 1 < n_groups:
            project(p + 1)
        if p > 0:
            out = out + out_part(p - 1)
        for hd in range(p * group, (p + 1) * group):
            recurrence_inputs(hd)
        scan(p)
    out = out + out_part(n_groups - 1)
    o_ref[...] = out.reshape(batch, t_blk, D_MODEL)


def _lru(x, g, w_in, conv_w, conv_b, w_ri, b_r, b_i, lam, w_out):
    batch, seq, _ = x.shape
    t_blk = ROW_TILE // batch
    halo = 64
    assert (LRU_CONV_WIDTH - 1) * batch <= halo and LRU_BLOCK == LANES
    blk = pl.BlockSpec((batch, t_blk, D_MODEL), lambda i: (0, i, 0))
    vec = _const_spec((1, D_MODEL))
    slab = lambda n: pltpu.VMEM((LRU_HEADS, n, LANES), F32)
    return pl.pallas_call(
        _lru_kernel,
        grid=(seq // t_blk,),
        in_specs=[blk, vec, _const_spec(w_in.shape), _const_spec(conv_w.shape), vec,
                  _const_spec(w_ri.shape), vec, vec, vec, _const_spec(w_out.shape)],
        out_specs=blk,
        out_shape=jax.ShapeDtypeStruct(x.shape, F32),
        scratch_shapes=[slab(halo + ROW_TILE), slab(ROW_TILE), slab(ROW_TILE),
                        pltpu.VMEM((ROW_TILE, D_MODEL), F32), slab(batch),
                        pltpu.VMEM((ROW_TILE, D_MODEL), BF16)],
        compiler_params=_params(1),
        name="rglru_mixer",
    )(x, g, w_in, conv_w, conv_b, w_ri, b_r, b_i, lam, w_out)


def kernel(x, mix_norm, mlp_norm, w_up, w_down, ab_w_in, ab_conv_w, ab_ln_g, ab_ln_b, ab_w_out,
           c_w_in, c_conv_w, c_conv_b, c_w_r, c_b_r, c_w_i, c_b_i, c_lambda, c_w_out, final_norm):
    batch, seq, d = x.shape
    vec = lambda a: a.reshape(1, -1).astype(F32)
    flat = lambda a: a.reshape(batch * seq, d)

    w_in = ab_w_in[0]
    o = 2 * CONV_CH
    w_q, w_k, w_v = (w_in[:, o + n * ATTN_WIDTH:o + (n + 1) * ATTN_WIDTH] for n in range(3))
    k, qt, vt = _qkv(x, vec(mix_norm[0]), w_k.astype(BF16), w_q.T.astype(BF16), w_v.T.astype(BF16))
    yb = _attention(qt, k, vt)
    conv_w = jnp.pad(ab_conv_w[0], ((0, 32 - CONV_WIDTH), (0, 0)))
    w_glu = w_in[:, 0:o].reshape(d, 2, CONV_CH // LANES, LANES).transpose(0, 2, 1, 3).reshape(d, o)
    x = _mix0(x, yb, vec(mix_norm[0]), w_glu.astype(BF16), conv_w, vec(ab_ln_g[0]), vec(ab_ln_b[0]),
              ab_w_out[0].astype(BF16))
    x = _mlp(flat(x), vec(mlp_norm[0]), w_up[0].astype(BF16), w_down[0].astype(BF16)).reshape(batch, seq, d)

    w_ri = jnp.concatenate([c_w_r[0], c_w_i[0]], axis=-1).astype(BF16)
    lru_conv_w = jnp.pad(c_conv_w[0], ((0, SUBLANES - LRU_CONV_WIDTH), (0, 0)))
    x = _lru(x, vec(mix_norm[1]), c_w_in[0].astype(BF16), lru_conv_w, vec(c_conv_b[0]), w_ri,
             vec(c_b_r[0]), vec(c_b_i[0]), vec(c_lambda[0]), c_w_out[0].astype(BF16))
    x = _mlp(flat(x), vec(mlp_norm[1]), w_up[1].astype(BF16), w_down[1].astype(BF16), vec(final_norm))
    return x.reshape(batch, seq, d)
```

```python
import functools

import jax
import jax.numpy as jnp
from jax import lax
from jax.experimental import pallas as pl
from jax.experimental.pallas import tpu as pltpu

F32 = jnp.float32
BF16 = jnp.bfloat16

EPS = 1e-6
D_MODEL = 1024
D_FF = 4 * D_MODEL
CONV_CH = 512
CONV_WIDTH = 31
ATTN_WIDTH = 512
HEAD_DIM = 64
N_HEADS = 8
MOBA_BLOCK = 256
MOBA_TOP_K = 3
LRU_HEADS = 8
LRU_BLOCK = 128
LRU_CONV_WIDTH = 4
LRU_C = 8.0

LANES = 128
SUBLANES = 8
BF16_ROWS = 16
HEADS_PER_STEP = LANES // HEAD_DIM
MASKED = -1e30
VMEM_LIMIT = 56 * 1024 * 1024

ROW_TILE = 512
MLP_CHUNK = 1024
MLP_SUB = 256
CONV_ROWS = 128


def _dot(a, b):
    return jnp.dot(a, b, preferred_element_type=F32)


def _dot_nt(a, b):
    return lax.dot_general(a, b, (((1,), (1,)), ((), ())), preferred_element_type=F32)


def _rmsnorm(x, g):
    return x * lax.rsqrt(jnp.mean(x * x, axis=-1, keepdims=True) + EPS) * g


def _const_spec(shape):
    zeros = (0,) * len(shape)
    return pl.BlockSpec(shape, lambda *_: zeros, pipeline_mode=pl.Buffered(1))


def _params(n_axes):
    return pltpu.CompilerParams(dimension_semantics=("arbitrary",) * n_axes,
                                vmem_limit_bytes=VMEM_LIMIT)


def _qkv_kernel(x_ref, g_ref, wk_ref, wqt_ref, wvt_ref, k_ref, qt_ref, vt_ref):
    h = _rmsnorm(x_ref[...], g_ref[...]).astype(BF16)
    k_ref[...] = _dot(h, wk_ref[...]).astype(BF16)
    qt_ref[...] = _dot_nt(wqt_ref[...], h).astype(BF16)
    vt_ref[...] = _dot_nt(wvt_ref[...], h).astype(BF16)


def _qkv(x, g, wk, wqt, wvt):
    batch, seq, _ = x.shape
    w_spec = _const_spec((D_MODEL, ATTN_WIDTH))
    wt_spec = _const_spec((ATTN_WIDTH, D_MODEL))
    t_spec = pl.BlockSpec((None, ATTN_WIDTH, ROW_TILE), lambda b, t: (b, 0, t))
    t_shape = jax.ShapeDtypeStruct((batch, ATTN_WIDTH, seq), BF16)
    return pl.pallas_call(
        _qkv_kernel,
        grid=(batch, seq // ROW_TILE),
        in_specs=[pl.BlockSpec((None, ROW_TILE, D_MODEL), lambda b, t: (b, t, 0)),
                  _const_spec((1, D_MODEL)), w_spec, wt_spec, wt_spec],
        out_specs=[pl.BlockSpec((None, ROW_TILE, ATTN_WIDTH), lambda b, t: (b, t, 0)), t_spec, t_spec],
        out_shape=[jax.ShapeDtypeStruct((batch, seq, ATTN_WIDTH), BF16), t_shape, t_shape],
        compiler_params=_params(2),
        name="qkv_proj",
    )(x, g, wk, wqt, wvt)


def _head_lanes(hh):
    qk0 = hh * HEAD_DIM
    ex0 = (1 - hh) * HEAD_DIM
    return qk0, ex0, ex0 + BF16_ROWS


V_ROWS = HEAD_DIM + BF16_ROWS


def _attn_prepare(k_ref, vt_ref, slope_ref, kaug_ref, kbar_ref, vta_ref):
    n_keys = k_ref.shape[0]
    rowk = lax.broadcasted_iota(jnp.int32, (n_keys, LANES), 0)
    lane = lax.broadcasted_iota(jnp.int32, (n_keys, LANES), 1)
    tk = (rowk & (MOBA_BLOCK - 1)).astype(F32)
    blk = rowk >> (MOBA_BLOCK.bit_length() - 1)
    ones_rows = jnp.where(lax.broadcasted_iota(jnp.int32, (BF16_ROWS, n_keys), 0) == 0, 1.0, 0.0)
    for h in range(N_HEADS):
        pair, hh = divmod(h, HEADS_PER_STEP)
        qk0, ex0, bias0 = _head_lanes(hh)
        slope = slope_ref[h:h + 1, 0:LANES]
        extra = jnp.where(lane == ex0, 1.0,
                          jnp.where(lane == ex0 + 1, slope * tk,
                                    jnp.where(lane - bias0 == blk, 1.0, 0.0)))
        is_k = (lane >= qk0) & (lane < qk0 + HEAD_DIM)
        kaug_ref[h] = jnp.where(is_k, k_ref[:, pair * LANES:(pair + 1) * LANES], extra.astype(BF16))
        vta_ref[h, 0:HEAD_DIM, :] = vt_ref[h * HEAD_DIM:(h + 1) * HEAD_DIM, :]
        vta_ref[h, HEAD_DIM:V_ROWS, :] = ones_rows.astype(BF16)
    for j in range(n_keys // MOBA_BLOCK):
        kj = k_ref[j * MOBA_BLOCK:(j + 1) * MOBA_BLOCK, :].astype(F32)
        kbar_ref[j:j + 1, :] = jnp.sum(kj, axis=0, keepdims=True) * (1.0 / MOBA_BLOCK)


def _attn_block(c, qt_ref, slope_ref, kaug_ref, kbar_ref, vta_ref, o_ref):
    rowi = lax.broadcasted_iota(jnp.int32, (SUBLANES, MOBA_BLOCK), 0)
    tq = lax.broadcasted_iota(jnp.int32, (SUBLANES, MOBA_BLOCK), 1).astype(F32)
    causal = (lax.broadcasted_iota(jnp.int32, (MOBA_BLOCK, MOBA_BLOCK), 0)
              <= lax.broadcasted_iota(jnp.int32, (MOBA_BLOCK, MOBA_BLOCK), 1))
    zeros8 = jnp.zeros((SUBLANES, MOBA_BLOCK), F32)
    lane_k = lax.broadcasted_iota(jnp.int32, (SUBLANES, LANES), 1)
    n_past = c * MOBA_BLOCK
    own = slice(n_past, n_past + MOBA_BLOCK)

    qats = []
    for pair in range(N_HEADS // HEADS_PER_STEP):
        qt = qt_ref[pair * LANES:(pair + 1) * LANES, :]
        if c > 0:
            kbar = kbar_ref[:, pair * LANES:(pair + 1) * LANES]
            kbar = jnp.concatenate([jnp.where(lane_k < HEAD_DIM, kbar, 0.0),
                                    jnp.where(lane_k >= HEAD_DIM, kbar, 0.0)], axis=0)
            kb_hi = kbar.astype(BF16)
            kb_lo = (kbar - kb_hi.astype(F32)).astype(BF16)
            gates = _dot(kb_hi, qt) + _dot(kb_lo, qt)
        for hh in range(HEADS_PER_STEP):
            h = pair * HEADS_PER_STEP + hh
            qk0, ex0, bias0 = _head_lanes(hh)
            slope = slope_ref[h:h + 1, :]
            if c > 0:
                gate = gates[hh * SUBLANES:(hh + 1) * SUBLANES, :]
                elig = rowi < c
                sel = elig & (jnp.abs(gate) < jnp.inf)
                if c > MOBA_TOP_K:
                    beaten = zeros8
                    for jp in range(c):
                        gp = gate[jp:jp + 1, :]
                        beats = (gp > gate) | ((gp == gate) & (rowi > jp))
                        beaten = beaten + jnp.where(beats, 1.0, 0.0)
                    sel = sel & (beaten < float(MOBA_TOP_K))
                dist = (c - rowi).astype(F32) * float(MOBA_BLOCK)
                bias = jnp.where(elig, jnp.where(sel, -slope * dist, MASKED), 0.0)
            else:
                bias = zeros8
            ext_a = jnp.where(rowi == 0, -slope * tq, jnp.where(rowi == 1, 1.0, 0.0))
            ext = jnp.concatenate([ext_a, zeros8, bias, zeros8], axis=0).astype(BF16)
            pad = jnp.zeros((HEAD_DIM - ext.shape[0], MOBA_BLOCK), BF16)
            qs = qt[qk0:qk0 + HEAD_DIM, :] * jnp.asarray(HEAD_DIM ** -0.5, BF16)
            qats.append(jnp.concatenate([qs, ext, pad] if hh == 0 else [ext, pad, qs], axis=0))

    def logits(h):
        s_own = jnp.where(causal, _dot(kaug_ref[h, own, :], qats[h]), -jnp.inf)
        m = jnp.max(s_own, axis=0, keepdims=True)
        s_past = None
        if c > 0:
            s_past = _dot(kaug_ref[h, 0:n_past, :], qats[h])
            m = jnp.maximum(m, jnp.max(s_past, axis=0, keepdims=True))
        return s_own, s_past, m

    def weighted_values(h, s_own, s_past, m):
        acc = _dot(vta_ref[h, :, own], jnp.exp(s_own - m).astype(BF16))
        if c > 0:
            acc = acc + _dot(vta_ref[h, :, 0:n_past], jnp.exp(s_past - m).astype(BF16))
        return acc[0:HEAD_DIM, :] * (1.0 / acc[HEAD_DIM:HEAD_DIM + 1, :])

    outs = []
    state = logits(0)
    for h in range(N_HEADS):
        nxt = logits(h + 1) if h + 1 < N_HEADS else None
        outs.append(weighted_values(h, *state))
        state = nxt
    for pair in range(N_HEADS // HEADS_PER_STEP):
        o_pair = jnp.concatenate(outs[pair * HEADS_PER_STEP:(pair + 1) * HEADS_PER_STEP], axis=0)
        o_ref[:, pair * LANES:(pair + 1) * LANES] = o_pair.T.astype(o_ref.dtype)


def _attn_kernel(slope_ref, qt_ref, k_ref, vt_ref, o_ref, kaug_ref, kbar_ref, vta_ref):
    i = pl.program_id(1)

    @pl.when(i == 0)
    def _():
        _attn_prepare(k_ref, vt_ref, slope_ref, kaug_ref, kbar_ref, vta_ref)

    for c in range(k_ref.shape[0] // MOBA_BLOCK):
        @pl.when(i == c)
        def _(c=c):
            _attn_block(c, qt_ref, slope_ref, kaug_ref, kbar_ref, vta_ref, o_ref)


def _attention(qt, k, vt):
    batch, seq, _ = k.shape
    n_blk = seq // MOBA_BLOCK
    slopes = 2.0 ** (-8.0 * jnp.arange(1, N_HEADS + 1, dtype=F32) / N_HEADS)
    slopes = jnp.broadcast_to(slopes.reshape(N_HEADS, 1), (N_HEADS, MOBA_BLOCK))
    return pl.pallas_call(
        _attn_kernel,
        grid=(batch, n_blk),
        in_specs=[
            _const_spec((N_HEADS, MOBA_BLOCK)),
            pl.BlockSpec((None, ATTN_WIDTH, MOBA_BLOCK), lambda b, i: (b, 0, i)),
            pl.BlockSpec((None, seq, ATTN_WIDTH), lambda b, i: (b, 0, 0)),
            pl.BlockSpec((None, ATTN_WIDTH, seq), lambda b, i: (b, 0, 0)),
        ],
        out_specs=pl.BlockSpec((None, MOBA_BLOCK, ATTN_WIDTH), lambda b, i: (b, i, 0)),
        out_shape=jax.ShapeDtypeStruct(k.shape, BF16),
        scratch_shapes=[pltpu.VMEM((N_HEADS, seq, LANES), BF16),
                        pltpu.VMEM((n_blk, ATTN_WIDTH), F32),
                        pltpu.VMEM((N_HEADS, V_ROWS, seq), BF16)],
        compiler_params=_params(2),
        name="moba_attention",
    )(slopes, qt, k, vt)


def _to_time_major(val, buf_ref, slab0, row0, batch, t_blk):
    for s in range(val.shape[1] // LANES):
        for b in range(batch):
            buf_ref[slab0 + s, pl.ds(row0 + b, t_blk, stride=batch), :] = (
                val[b * t_blk:(b + 1) * t_blk, s * LANES:(s + 1) * LANES])


def _from_time_major(buf_ref, s, b, batch, t_blk):
    return buf_ref[s, pl.ds(b, t_blk, stride=batch), :]


def _mix0_kernel(x_ref, yb_ref, g_ref, wz_ref, cw_ref, lg_ref, lb_ref, w_ref, mg_ref, wu_ref, wd_ref,
                 o_ref, zbuf_ref, conv_ref, lhs_ref, xmid_ref, hm_ref, act_ref):
    i = pl.program_id(0)
    batch, t_blk, _ = x_ref.shape
    rows = batch * t_blk
    n_slab = CONV_CH // LANES
    slot = i % 2

    @pl.when(i == 0)
    def _():
        zbuf_ref[:, 0:rows, :] = jnp.zeros((n_slab, rows, LANES), F32)
        xmid_ref[1] = jnp.zeros((rows, D_MODEL), F32)

    @pl.when(i > 0)
    def _():
        zbuf_ref[:, 0:rows, :] = zbuf_ref[:, rows:2 * rows, :]

    x = x_ref[...].reshape(rows, D_MODEL)
    h = _rmsnorm(x, g_ref[...]).astype(BF16)

    glu_in = {}
    stats = {}
    mine = xmid_ref.at[slot]

    def project(s):
        glu_in[s] = _dot(h, wz_ref[:, 2 * s * LANES:2 * (s + 1) * LANES])

    def conv(s):
        u = glu_in.pop(s)
        z = u[:, 0:LANES] * jax.nn.sigmoid(u[:, LANES:2 * LANES])
        _to_time_major(z, zbuf_ref, s, rows, batch, t_blk)
        for r0 in range(0, rows, CONV_ROWS):
            acc = jnp.zeros((CONV_ROWS, LANES), F32)
            for j in range(CONV_WIDTH):
                start = r0 + rows - (CONV_WIDTH - 1 - j) * batch
                acc = acc + zbuf_ref[s, start:start + CONV_ROWS, :] * cw_ref[j:j + 1, s * LANES:(s + 1) * LANES]
            conv_ref[s, r0:r0 + CONV_ROWS, :] = acc

    def attention_part(j):
        cols = slice(j * CONV_CH, (j + 1) * CONV_CH)
        yb = yb_ref[...].reshape(rows, ATTN_WIDTH)
        mine[:, cols] = x_ref[:, :, cols].reshape(rows, CONV_CH) + _dot(yb, w_ref[CONV_CH:CONV_CH + ATTN_WIDTH, cols])

    def layernorm_stats():
        y = [conv_ref[s] for s in range(n_slab)]
        mu = jnp.sum(y[0] + y[1] + y[2] + y[3], axis=-1, keepdims=True) * (1.0 / CONV_CH)
        d = [ys - mu for ys in y]
        var = jnp.sum(d[0] * d[0] + d[1] * d[1] + d[2] * d[2] + d[3] * d[3], axis=-1, keepdims=True) * (1.0 / CONV_CH)
        stats["mu"], stats["inv"] = mu, lax.rsqrt(var + EPS)

    def normalize(s):
        cols = slice(s * LANES, (s + 1) * LANES)
        yn = (conv_ref[s] - stats["mu"]) * stats["inv"] * lg_ref[:, cols] + lb_ref[:, cols]
        conv_ref[s] = yn * jax.nn.sigmoid(yn)
        for b in range(batch):
            lhs_ref[b * t_blk:(b + 1) * t_blk, cols] = _from_time_major(conv_ref, s, b, batch, t_blk).astype(BF16)

    def conv_part(first):
        k_rows = slice(first * LANES, (first + 2) * LANES)
        mine[...] = mine[...] + _dot(lhs_ref[:, k_rows], w_ref[k_rows, :])

    mlp = _mlp_stages(xmid_ref.at[1 - slot], mg_ref, wu_ref, wd_ref, None, o_ref, hm_ref, act_ref)
    dense = [functools.partial(stage, c, j) for c in range(mlp.n_chunks)
             for stage in (mlp.up, mlp.down) for j in range(mlp.n_sub)]
    mixer_after = {0: lambda: project(1), 1: lambda: conv(0), 4: lambda: project(2), 5: lambda: conv(1),
                   8: lambda: project(3), 9: lambda: conv(2), 12: lambda: attention_part(0),
                   13: lambda: conv(3), 16: lambda: attention_part(1), 18: layernorm_stats,
                   20: lambda: normalize(0), 22: lambda: normalize(1), 24: lambda: conv_part(0),
                   26: lambda: normalize(2), 28: lambda: normalize(3), 30: lambda: conv_part(2)}
    assert max(mixer_after) < len(dense)
    mlp.norm()
    project(0)
    for k, dense_stage in enumerate(dense):
        dense_stage()
        if k in mixer_after:
            mixer_after[k]()


def _layer0_tail(x, yb, g, w_z, conv_w, ln_g, ln_b, w_out, mlp_g, w_up, w_down):
    batch, seq, _ = x.shape
    t_blk = ROW_TILE // batch
    n_blk = seq // t_blk
    assert (CONV_WIDTH - 1) * batch <= ROW_TILE and CONV_CH == 4 * LANES
    blk = lambda width: pl.BlockSpec((batch, t_blk, width), lambda i: (0, jnp.minimum(i, n_blk - 1), 0))
    vec = _const_spec((1, CONV_CH))
    return pl.pallas_call(
        _mix0_kernel,
        grid=(n_blk + 1,),
        in_specs=[blk(D_MODEL), blk(ATTN_WIDTH), _const_spec((1, D_MODEL)), _const_spec(w_z.shape),
                  _const_spec(conv_w.shape), vec, vec, _const_spec(w_out.shape),
                  _const_spec((1, D_MODEL)), _const_spec(w_up.shape), _const_spec(w_down.shape)],
        out_specs=pl.BlockSpec((batch, t_blk, D_MODEL), lambda i: (0, jnp.maximum(i - 1, 0), 0)),
        out_shape=jax.ShapeDtypeStruct(x.shape, F32),
        scratch_shapes=[pltpu.VMEM((CONV_CH // LANES, 2 * ROW_TILE, LANES), F32),
                        pltpu.VMEM((CONV_CH // LANES, ROW_TILE, LANES), F32),
                        pltpu.VMEM((ROW_TILE, CONV_CH), BF16),
                        pltpu.VMEM((2, ROW_TILE, D_MODEL), F32),
                        pltpu.VMEM((ROW_TILE, D_MODEL), BF16),
                        pltpu.VMEM((2, ROW_TILE, MLP_CHUNK), BF16)],
        compiler_params=_params(1),
        name="layer0_tail",
    )(x, yb, g, w_z, conv_w, ln_g, ln_b, w_out, mlp_g, w_up, w_down)


class _mlp_stages:
    n_chunks = D_FF // MLP_CHUNK
    n_sub = MLP_CHUNK // MLP_SUB

    def __init__(self, x_ref, g_ref, wu_ref, wd_ref, final_g_ref, o_ref, h_ref, act_ref):
        self.x_ref, self.g_ref, self.wu_ref, self.wd_ref = x_ref, g_ref, wu_ref, wd_ref
        self.final_g_ref, self.o_ref, self.h_ref, self.act_ref = final_g_ref, o_ref, h_ref, act_ref

    def norm(self):
        self.h_ref[...] = _rmsnorm(self.x_ref[...], self.g_ref[...]).astype(BF16)

    def up(self, c, j):
        cols = slice(j * MLP_SUB, (j + 1) * MLP_SUB)
        a = _dot(self.h_ref[...], self.wu_ref[:, c * MLP_CHUNK + j * MLP_SUB:c * MLP_CHUNK + (j + 1) * MLP_SUB])
        self.act_ref[c % 2, :, cols] = jnp.square(jnp.maximum(a, 0.0)).astype(BF16)

    def down(self, c, j):
        cols = slice(j * MLP_SUB, (j + 1) * MLP_SUB)
        batch, t_blk, _ = self.o_ref.shape
        part = _dot(self.act_ref[c % 2], self.wd_ref[c * MLP_CHUNK:(c + 1) * MLP_CHUNK, cols])
        acc = self.x_ref[:, cols] if c == 0 else self.o_ref[:, :, cols].reshape(part.shape)
        self.o_ref[:, :, cols] = (acc + part).reshape(batch, t_blk, MLP_SUB)

    def finish(self):
        if self.final_g_ref is not None:
            shape = self.o_ref.shape
            acc = self.o_ref[...].reshape(shape[0] * shape[1], shape[2])
            self.o_ref[...] = _rmsnorm(acc, self.final_g_ref[...]).reshape(shape)


def _gelu_tanh(x):
    return 0.5 * x * (1.0 + jnp.tanh(0.7978845608028654 * (x + 0.044715 * (x * x * x))))


def _lru_kernel(x_ref, g_ref, win_ref, cw_ref, cb_ref, wri_ref, br_ref, bi_ref, lam_ref, wout_ref,
                mg_ref, wu_ref, wd_ref, fg_ref,
                o_ref, xbuf_ref, a_ref, b_ref, gate_ref, h_ref, y_ref, xmid_ref, hm_ref, act_ref):
    i = pl.program_id(0)
    batch, t_blk, _ = x_ref.shape
    rows = batch * t_blk
    halo = xbuf_ref.shape[1] - rows
    slot = i % 2

    @pl.when(i == 0)
    def _():
        xbuf_ref[:, 0:halo, :] = jnp.zeros((LRU_HEADS, halo, LANES), F32)
        h_ref[...] = jnp.zeros(h_ref.shape, F32)
        xmid_ref[1] = jnp.zeros((rows, D_MODEL), F32)

    @pl.when(i > 0)
    def _():
        xbuf_ref[:, 0:halo, :] = xbuf_ref[:, rows:rows + halo, :]

    x = x_ref[...].reshape(rows, D_MODEL)
    h = _rmsnorm(x, g_ref[...]).astype(BF16)
    neg_lam = -lam_ref[...]
    softplus = jnp.maximum(neg_lam, 0.0) + jnp.log1p(jnp.exp(-jnp.abs(neg_lam)))
    group = 2
    width = group * LRU_BLOCK

    def project_gate(p):
        cols = slice(p * width, (p + 1) * width)
        gate_ref[:, cols] = _gelu_tanh(_dot(h, win_ref[:, cols]))

    def project_x(p):
        xr = _dot(h, win_ref[:, D_MODEL + p * width:D_MODEL + (p + 1) * width])
        _to_time_major(xr, xbuf_ref, p * group, halo, batch, t_blk)

    def recurrence_inputs(hd):
        cols = slice(hd * LRU_BLOCK, (hd + 1) * LRU_BLOCK)
        xc = cb_ref[:, cols]
        for j in range(LRU_CONV_WIDTH):
            start = halo - (LRU_CONV_WIDTH - 1 - j) * batch
            xc = xc + xbuf_ref[hd, start:start + rows, :] * cw_ref[j:j + 1, cols]
        ri = _dot(xc.astype(BF16), wri_ref[hd])
        r = jax.nn.sigmoid(ri[:, 0:LRU_BLOCK] + br_ref[:, cols])
        ig = jax.nn.sigmoid(ri[:, LRU_BLOCK:2 * LRU_BLOCK] + bi_ref[:, cols])
        log_a = -LRU_C * r * softplus[:, cols]
        a = jnp.exp(log_a)
        a_ref[hd] = a
        b_ref[hd] = jnp.sqrt(-jnp.tanh(log_a) * (a * a + 1.0)) * (ig * xc)

    def scan(p):
        heads = range(p * group, (p + 1) * group)
        state = {hd: h_ref[hd] for hd in heads}
        for t in range(t_blk):
            step = slice(t * batch, (t + 1) * batch)
            for hd in heads:
                state[hd] = a_ref[hd, step, :] * state[hd] + b_ref[hd, step, :]
                b_ref[hd, step, :] = state[hd]
        for hd in heads:
            h_ref[hd] = state[hd]
            cols = slice(hd * LRU_BLOCK, (hd + 1) * LRU_BLOCK)
            for b in range(batch):
                blk = slice(b * t_blk, (b + 1) * t_blk)
                y_ref[blk, cols] = (_from_time_major(b_ref, hd, b, batch, t_blk) * gate_ref[blk, cols]).astype(BF16)

    def out_part(p):
        k_rows = slice(p * width, (p + 1) * width)
        return _dot(y_ref[:, k_rows], wout_ref[k_rows, :])

    n_groups = LRU_HEADS // group
    mlp = _mlp_stages(xmid_ref.at[1 - slot], mg_ref, wu_ref, wd_ref, fg_ref, o_ref, hm_ref, act_ref)
    assert n_groups == mlp.n_chunks and mlp.n_sub == 4 and group == 2
    out = [x]

    def add_out_part(p):
        out[0] = out[0] + out_part(p)

    mlp.norm()
    project_x(0)
    project_gate(0)
    for p in range(n_groups):
        last = p + 1 == n_groups
        mixer = [lambda: recurrence_inputs(p * group),
                 lambda: recurrence_inputs(p * group + 1),
                 (lambda: None) if last else (lambda: project_x(p + 1)),
                 (lambda: None) if last else (lambda: project_gate(p + 1)),
                 lambda: scan(p),
                 (lambda: add_out_part(p - 1)) if p > 0 else (lambda: None)]
        dense = [lambda j=j: mlp.up(p, j) for j in range(mlp.n_sub)]
        dense += [lambda j=j: mlp.down(p, j) for j in range(mlp.n_sub)]
        for k, dense_stage in enumerate(dense):
            dense_stage()
            if k < len(mixer):
                mixer[k]()
    xmid_ref[slot] = out[0] + out_part(n_groups - 1)
    mlp.finish()


def _layer1(x, g, w_in, conv_w, conv_b, w_ri, b_r, b_i, lam, w_out, mlp_g, w_up, w_down, final_g):
    batch, seq, _ = x.shape
    t_blk = ROW_TILE // batch
    n_blk = seq // t_blk
    halo = 64
    assert (LRU_CONV_WIDTH - 1) * batch <= halo and LRU_BLOCK == LANES
    vec = _const_spec((1, D_MODEL))
    slab = lambda n: pltpu.VMEM((LRU_HEADS, n, LANES), F32)
    return pl.pallas_call(
        _lru_kernel,
        grid=(n_blk + 1,),
        in_specs=[pl.BlockSpec((batch, t_blk, D_MODEL), lambda i: (0, jnp.minimum(i, n_blk - 1), 0)),
                  vec, _const_spec(w_in.shape), _const_spec(conv_w.shape), vec,
                  _const_spec(w_ri.shape), vec, vec, vec, _const_spec(w_out.shape),
                  vec, _const_spec(w_up.shape), _const_spec(w_down.shape), vec],
        out_specs=pl.BlockSpec((batch, t_blk, D_MODEL), lambda i: (0, jnp.maximum(i - 1, 0), 0)),
        out_shape=jax.ShapeDtypeStruct(x.shape, F32),
        scratch_shapes=[slab(halo + ROW_TILE), slab(ROW_TILE), slab(ROW_TILE),
                        pltpu.VMEM((ROW_TILE, D_MODEL), F32), slab(batch),
                        pltpu.VMEM((ROW_TILE, D_MODEL), BF16),
                        pltpu.VMEM((2, ROW_TILE, D_MODEL), F32),
                        pltpu.VMEM((ROW_TILE, D_MODEL), BF16),
                        pltpu.VMEM((2, ROW_TILE, MLP_CHUNK), BF16)],
        compiler_params=_params(1),
        name="layer1",
    )(x, g, w_in, conv_w, conv_b, w_ri, b_r, b_i, lam, w_out, mlp_g, w_up, w_down, final_g)


def kernel(x, mix_norm, mlp_norm, w_up, w_down, ab_w_in, ab_conv_w, ab_ln_g, ab_ln_b, ab_w_out,
           c_w_in, c_conv_w, c_conv_b, c_w_r, c_b_r, c_w_i, c_b_i, c_lambda, c_w_out, final_norm):
    batch, seq, d = x.shape
    vec = lambda a: a.reshape(1, -1).astype(F32)

    w_in = ab_w_in[0]
    o = 2 * CONV_CH
    w_q, w_k, w_v = (w_in[:, o + n * ATTN_WIDTH:o + (n + 1) * ATTN_WIDTH] for n in range(3))
    k, qt, vt = _qkv(x, vec(mix_norm[0]), w_k.astype(BF16), w_q.T.astype(BF16), w_v.T.astype(BF16))
    yb = _attention(qt, k, vt)
    conv_w = jnp.pad(ab_conv_w[0], ((0, 32 - CONV_WIDTH), (0, 0)))
    w_glu = w_in[:, 0:o].reshape(d, 2, CONV_CH // LANES, LANES).transpose(0, 2, 1, 3).reshape(d, o)
    x = _layer0_tail(x, yb, vec(mix_norm[0]), w_glu.astype(BF16), conv_w, vec(ab_ln_g[0]), vec(ab_ln_b[0]),
                     ab_w_out[0].astype(BF16), vec(mlp_norm[0]), w_up[0].astype(BF16), w_down[0].astype(BF16))

    w_ri = jnp.concatenate([c_w_r[0], c_w_i[0]], axis=-1).astype(BF16)
    lru_conv_w = jnp.pad(c_conv_w[0], ((0, SUBLANES - LRU_CONV_WIDTH), (0, 0)))
    return _layer1(x, vec(mix_norm[1]), c_w_in[0].astype(BF16), lru_conv_w, vec(c_conv_b[0]), w_ri,
                   vec(c_b_r[0]), vec(c_b_i[0]), vec(c_lambda[0]), c_w_out[0].astype(BF16),
                   vec(mlp_norm[1]), w_up[1].astype(BF16), w_down[1].astype(BF16), vec(final_norm))
```

```python
import functools

import jax
import jax.numpy as jnp
from jax import lax
from jax.experimental import pallas as pl
from jax.experimental.pallas import tpu as pltpu

F32 = jnp.float32
BF16 = jnp.bfloat16

EPS = 1e-6
D_MODEL = 1024
D_FF = 4 * D_MODEL
CONV_CH = 512
CONV_WIDTH = 31
ATTN_WIDTH = 512
HEAD_DIM = 64
N_HEADS = 8
MOBA_BLOCK = 256
MOBA_TOP_K = 3
LRU_HEADS = 8
LRU_BLOCK = 128
LRU_CONV_WIDTH = 4
LRU_C = 8.0

LANES = 128
SUBLANES = 8
BF16_ROWS = 16
HEADS_PER_STEP = LANES // HEAD_DIM
MASKED = -1e30
VMEM_LIMIT = 56 * 1024 * 1024

ROW_TILE = 512
MLP_CHUNK = 1024
MLP_SUB = 256
CONV_ROWS = 128


def _dot(a, b):
    return jnp.dot(a, b, preferred_element_type=F32)


def _dot_nt(a, b):
    return lax.dot_general(a, b, (((1,), (1,)), ((), ())), preferred_element_type=F32)


def _rmsnorm(x, g):
    return x * lax.rsqrt(jnp.mean(x * x, axis=-1, keepdims=True) + EPS) * g


def _const_spec(shape):
    zeros = (0,) * len(shape)
    return pl.BlockSpec(shape, lambda *_: zeros, pipeline_mode=pl.Buffered(1))


def _params(n_axes):
    return pltpu.CompilerParams(dimension_semantics=("arbitrary",) * n_axes,
                                vmem_limit_bytes=VMEM_LIMIT)


def _qkv_kernel(x_ref, g_ref, wk_ref, wqt_ref, wvt_ref, k_ref, qt_ref, vt_ref):
    h = _rmsnorm(x_ref[...], g_ref[...]).astype(BF16)
    k_ref[...] = _dot(h, wk_ref[...]).astype(BF16)
    qt_ref[...] = _dot_nt(wqt_ref[...], h).astype(BF16)
    vt_ref[...] = _dot_nt(wvt_ref[...], h).astype(BF16)


def _qkv(x, g, wk, wqt, wvt):
    batch, seq, _ = x.shape
    w_spec = _const_spec((D_MODEL, ATTN_WIDTH))
    wt_spec = _const_spec((ATTN_WIDTH, D_MODEL))
    t_spec = pl.BlockSpec((None, ATTN_WIDTH, ROW_TILE), lambda b, t: (b, 0, t))
    t_shape = jax.ShapeDtypeStruct((batch, ATTN_WIDTH, seq), BF16)
    return pl.pallas_call(
        _qkv_kernel,
        grid=(batch, seq // ROW_TILE),
        in_specs=[pl.BlockSpec((None, ROW_TILE, D_MODEL), lambda b, t: (b, t, 0)),
                  _const_spec((1, D_MODEL)), w_spec, wt_spec, wt_spec],
        out_specs=[pl.BlockSpec((None, ROW_TILE, ATTN_WIDTH), lambda b, t: (b, t, 0)), t_spec, t_spec],
        out_shape=[jax.ShapeDtypeStruct((batch, seq, ATTN_WIDTH), BF16), t_shape, t_shape],
        compiler_params=_params(2),
        name="qkv_proj",
    )(x, g, wk, wqt, wvt)


def _head_lanes(hh):
    qk0 = hh * HEAD_DIM
    ex0 = (1 - hh) * HEAD_DIM
    return qk0, ex0, ex0 + BF16_ROWS


V_ROWS = HEAD_DIM + BF16_ROWS


def _attn_prepare(k_ref, vt_ref, slope_ref, kaug_ref, kbar_ref, vta_ref):
    n_keys = k_ref.shape[0]
    rowk = lax.broadcasted_iota(jnp.int32, (n_keys, LANES), 0)
    lane = lax.broadcasted_iota(jnp.int32, (n_keys, LANES), 1)
    tk = (rowk & (MOBA_BLOCK - 1)).astype(F32)
    blk = rowk >> (MOBA_BLOCK.bit_length() - 1)
    ones_rows = jnp.where(lax.broadcasted_iota(jnp.int32, (BF16_ROWS, n_keys), 0) == 0, 1.0, 0.0)
    for h in range(N_HEADS):
        pair, hh = divmod(h, HEADS_PER_STEP)
        qk0, ex0, bias0 = _head_lanes(hh)
        slope = slope_ref[h:h + 1, 0:LANES]
        extra = jnp.where(lane == ex0, 1.0,
                          jnp.where(lane == ex0 + 1, slope * tk,
                                    jnp.where(lane - bias0 == blk, 1.0, 0.0)))
        is_k = (lane >= qk0) & (lane < qk0 + HEAD_DIM)
        kaug_ref[h] = jnp.where(is_k, k_ref[:, pair * LANES:(pair + 1) * LANES], extra.astype(BF16))
        vta_ref[h, 0:HEAD_DIM, :] = vt_ref[h * HEAD_DIM:(h + 1) * HEAD_DIM, :]
        vta_ref[h, HEAD_DIM:V_ROWS, :] = ones_rows.astype(BF16)
    for j in range(n_keys // MOBA_BLOCK):
        kj = k_ref[j * MOBA_BLOCK:(j + 1) * MOBA_BLOCK, :].astype(F32)
        kbar_ref[j:j + 1, :] = jnp.sum(kj, axis=0, keepdims=True) * (1.0 / MOBA_BLOCK)


def _attn_block(c, qt_ref, slope_ref, kaug_ref, kbar_ref, vta_ref, o_ref):
    rowi = lax.broadcasted_iota(jnp.int32, (SUBLANES, MOBA_BLOCK), 0)
    tq = lax.broadcasted_iota(jnp.int32, (SUBLANES, MOBA_BLOCK), 1).astype(F32)
    causal = (lax.broadcasted_iota(jnp.int32, (MOBA_BLOCK, MOBA_BLOCK), 0)
              <= lax.broadcasted_iota(jnp.int32, (MOBA_BLOCK, MOBA_BLOCK), 1))
    zeros8 = jnp.zeros((SUBLANES, MOBA_BLOCK), F32)
    lane_k = lax.broadcasted_iota(jnp.int32, (SUBLANES, LANES), 1)
    n_past = c * MOBA_BLOCK
    own = slice(n_past, n_past + MOBA_BLOCK)

    qats = []
    for pair in range(N_HEADS // HEADS_PER_STEP):
        qt = qt_ref[pair * LANES:(pair + 1) * LANES, :]
        if c > 0:
            kbar = kbar_ref[:, pair * LANES:(pair + 1) * LANES]
            kbar = jnp.concatenate([jnp.where(lane_k < HEAD_DIM, kbar, 0.0),
                                    jnp.where(lane_k >= HEAD_DIM, kbar, 0.0)], axis=0)
            kb_hi = kbar.astype(BF16)
            kb_lo = (kbar - kb_hi.astype(F32)).astype(BF16)
            gates = _dot(kb_hi, qt) + _dot(kb_lo, qt)
        for hh in range(HEADS_PER_STEP):
            h = pair * HEADS_PER_STEP + hh
            qk0, ex0, bias0 = _head_lanes(hh)
            slope = slope_ref[h:h + 1, :]
            if c > 0:
                gate = gates[hh * SUBLANES:(hh + 1) * SUBLANES, :]
                elig = rowi < c
                sel = elig & (jnp.abs(gate) < jnp.inf)
                if c > MOBA_TOP_K:
                    beaten = zeros8
                    for jp in range(c):
                        gp = gate[jp:jp + 1, :]
                        beats = (gp > gate) | ((gp == gate) & (rowi > jp))
                        beaten = beaten + jnp.where(beats, 1.0, 0.0)
                    sel = sel & (beaten < float(MOBA_TOP_K))
                dist = (c - rowi).astype(F32) * float(MOBA_BLOCK)
                bias = jnp.where(elig, jnp.where(sel, -slope * dist, MASKED), 0.0)
            else:
                bias = zeros8
            ext_a = jnp.where(rowi == 0, -slope * tq, jnp.where(rowi == 1, 1.0, 0.0))
            ext = jnp.concatenate([ext_a, zeros8, bias, zeros8], axis=0).astype(BF16)
            pad = jnp.zeros((HEAD_DIM - ext.shape[0], MOBA_BLOCK), BF16)
            qs = qt[qk0:qk0 + HEAD_DIM, :] * jnp.asarray(HEAD_DIM ** -0.5, BF16)
            qats.append(jnp.concatenate([qs, ext, pad] if hh == 0 else [ext, pad, qs], axis=0))

    def logits(h):
        s_own = jnp.where(causal, _dot(kaug_ref[h, own, :], qats[h]), -jnp.inf)
        m = jnp.max(s_own, axis=0, keepdims=True)
        s_past = None
        if c > 0:
            s_past = _dot(kaug_ref[h, 0:n_past, :], qats[h])
            m = jnp.maximum(m, jnp.max(s_past, axis=0, keepdims=True))
        return s_own, s_past, m

    def weighted_values(h, s_own, s_past, m):
        acc = _dot(vta_ref[h, :, own], jnp.exp(s_own - m).astype(BF16))
        if c > 0:
            acc = acc + _dot(vta_ref[h, :, 0:n_past], jnp.exp(s_past - m).astype(BF16))
        return acc[0:HEAD_DIM, :] * (1.0 / acc[HEAD_DIM:HEAD_DIM + 1, :])

    outs = []
    state = logits(0)
    for h in range(N_HEADS):
        nxt = logits(h + 1) if h + 1 < N_HEADS else None
        outs.append(weighted_values(h, *state))
        state = nxt
    for pair in range(N_HEADS // HEADS_PER_STEP):
        o_pair = jnp.concatenate(outs[pair * HEADS_PER_STEP:(pair + 1) * HEADS_PER_STEP], axis=0)
        o_ref[:, pair * LANES:(pair + 1) * LANES] = o_pair.T.astype(o_ref.dtype)


def _attn_kernel(slope_ref, qt_ref, k_ref, vt_ref, o_ref, kaug_ref, kbar_ref, vta_ref):
    _attn_prepare(k_ref, vt_ref, slope_ref, kaug_ref, kbar_ref, vta_ref)
    for c in range(k_ref.shape[0] // MOBA_BLOCK):
        blk = slice(c * MOBA_BLOCK, (c + 1) * MOBA_BLOCK)
        _attn_block(c, qt_ref.at[:, blk], slope_ref, kaug_ref, kbar_ref, vta_ref, o_ref.at[blk])


def _attention(qt, k, vt):
    batch, seq, _ = k.shape
    n_blk = seq // MOBA_BLOCK
    slopes = 2.0 ** (-8.0 * jnp.arange(1, N_HEADS + 1, dtype=F32) / N_HEADS)
    slopes = jnp.broadcast_to(slopes.reshape(N_HEADS, 1), (N_HEADS, MOBA_BLOCK))
    return pl.pallas_call(
        _attn_kernel,
        grid=(batch,),
        in_specs=[
            _const_spec((N_HEADS, MOBA_BLOCK)),
            pl.BlockSpec((None, ATTN_WIDTH, seq), lambda b: (b, 0, 0)),
            pl.BlockSpec((None, seq, ATTN_WIDTH), lambda b: (b, 0, 0)),
            pl.BlockSpec((None, ATTN_WIDTH, seq), lambda b: (b, 0, 0)),
        ],
        out_specs=pl.BlockSpec((None, seq, ATTN_WIDTH), lambda b: (b, 0, 0)),
        out_shape=jax.ShapeDtypeStruct((batch, seq, ATTN_WIDTH), BF16),
        scratch_shapes=[pltpu.VMEM((N_HEADS, seq, LANES), BF16),
                        pltpu.VMEM((n_blk, ATTN_WIDTH), F32),
                        pltpu.VMEM((N_HEADS, V_ROWS, seq), BF16)],
        compiler_params=_params(1),
        name="moba_attention",
    )(slopes, qt, k, vt)


def _to_time_major(val, buf_ref, slab0, row0, batch, t_blk):
    for s in range(val.shape[1] // LANES):
        for b in range(batch):
            buf_ref[slab0 + s, pl.ds(row0 + b, t_blk, stride=batch), :] = (
                val[b * t_blk:(b + 1) * t_blk, s * LANES:(s + 1) * LANES])


def _from_time_major(buf_ref, s, b, batch, t_blk):
    return buf_ref[s, pl.ds(b, t_blk, stride=batch), :]


def _mix0_kernel(x_ref, yb_ref, g_ref, wz_ref, cw_ref, lg_ref, lb_ref, w_ref, mg_ref, wu_ref, wd_ref,
                 o_ref, zbuf_ref, conv_ref, lhs_ref, xmid_ref, hm_ref, act_ref):
    i = pl.program_id(0)
    batch, t_blk, _ = x_ref.shape
    rows = batch * t_blk
    n_slab = CONV_CH // LANES
    slot = i % 2

    @pl.when(i == 0)
    def _():
        zbuf_ref[:, 0:rows, :] = jnp.zeros((n_slab, rows, LANES), F32)
        xmid_ref[1] = jnp.zeros((rows, D_MODEL), F32)

    @pl.when(i > 0)
    def _():
        zbuf_ref[:, 0:rows, :] = zbuf_ref[:, rows:2 * rows, :]

    x = x_ref[...].reshape(rows, D_MODEL)
    h = _rmsnorm(x, g_ref[...]).astype(BF16)

    glu_in = {}
    stats = {}
    mine = xmid_ref.at[slot]

    def project(s):
        glu_in[s] = _dot(h, wz_ref[:, 2 * s * LANES:2 * (s + 1) * LANES])

    def conv(s):
        u = glu_in.pop(s)
        z = u[:, 0:LANES] * jax.nn.sigmoid(u[:, LANES:2 * LANES])
        _to_time_major(z, zbuf_ref, s, rows, batch, t_blk)
        for r0 in range(0, rows, CONV_ROWS):
            acc = jnp.zeros((CONV_ROWS, LANES), F32)
            for j in range(CONV_WIDTH):
                start = r0 + rows - (CONV_WIDTH - 1 - j) * batch
                acc = acc + zbuf_ref[s, start:start + CONV_ROWS, :] * cw_ref[j:j + 1, s * LANES:(s + 1) * LANES]
            conv_ref[s, r0:r0 + CONV_ROWS, :] = acc

    def attention_part(j):
        cols = slice(j * CONV_CH, (j + 1) * CONV_CH)
        yb = yb_ref[...].reshape(rows, ATTN_WIDTH)
        mine[:, cols] = x_ref[:, :, cols].reshape(rows, CONV_CH) + _dot(yb, w_ref[CONV_CH:CONV_CH + ATTN_WIDTH, cols])

    def layernorm_stats():
        y = [conv_ref[s] for s in range(n_slab)]
        mu = jnp.sum(y[0] + y[1] + y[2] + y[3], axis=-1, keepdims=True) * (1.0 / CONV_CH)
        d = [ys - mu for ys in y]
        var = jnp.sum(d[0] * d[0] + d[1] * d[1] + d[2] * d[2] + d[3] * d[3], axis=-1, keepdims=True) * (1.0 / CONV_CH)
        stats["mu"], stats["inv"] = mu, lax.rsqrt(var + EPS)

    def normalize(s):
        cols = slice(s * LANES, (s + 1) * LANES)
        yn = (conv_ref[s] - stats["mu"]) * stats["inv"] * lg_ref[:, cols] + lb_ref[:, cols]
        conv_ref[s] = yn * jax.nn.sigmoid(yn)
        for b in range(batch):
            lhs_ref[b * t_blk:(b + 1) * t_blk, cols] = _from_time_major(conv_ref, s, b, batch, t_blk).astype(BF16)

    def conv_part(first):
        k_rows = slice(first * LANES, (first + 2) * LANES)
        mine[...] = mine[...] + _dot(lhs_ref[:, k_rows], w_ref[k_rows, :])

    mlp = _mlp_stages(xmid_ref.at[1 - slot], mg_ref, wu_ref, wd_ref, None, o_ref, hm_ref, act_ref)
    dense = [functools.partial(stage, c, j) for c in range(mlp.n_chunks)
             for stage in (mlp.up, mlp.down) for j in range(mlp.n_sub)]
    mixer_after = {0: lambda: project(1), 1: lambda: conv(0), 4: lambda: project(2), 5: lambda: conv(1),
                   8: lambda: project(3), 9: lambda: conv(2), 12: lambda: attention_part(0),
                   13: lambda: conv(3), 16: lambda: attention_part(1), 18: layernorm_stats,
                   20: lambda: normalize(0), 22: lambda: normalize(1), 24: lambda: conv_part(0),
                   26: lambda: normalize(2), 28: lambda: normalize(3), 30: lambda: conv_part(2)}
    assert max(mixer_after) < len(dense)
    mlp.norm()
    project(0)
    for k, dense_stage in enumerate(dense):
        dense_stage()
        if k in mixer_after:
            mixer_after[k]()


def _layer0_tail(x, yb, g, w_z, conv_w, ln_g, ln_b, w_out, mlp_g, w_up, w_down):
    batch, seq, _ = x.shape
    t_blk = ROW_TILE // batch
    n_blk = seq // t_blk
    assert (CONV_WIDTH - 1) * batch <= ROW_TILE and CONV_CH == 4 * LANES
    blk = lambda width: pl.BlockSpec((batch, t_blk, width), lambda i: (0, jnp.minimum(i, n_blk - 1), 0))
    vec = _const_spec((1, CONV_CH))
    return pl.pallas_call(
        _mix0_kernel,
        grid=(n_blk + 1,),
        in_specs=[blk(D_MODEL), blk(ATTN_WIDTH), _const_spec((1, D_MODEL)), _const_spec(w_z.shape),
                  _const_spec(conv_w.shape), vec, vec, _const_spec(w_out.shape),
                  _const_spec((1, D_MODEL)), _const_spec(w_up.shape), _const_spec(w_down.shape)],
        out_specs=pl.BlockSpec((batch, t_blk, D_MODEL), lambda i: (0, jnp.maximum(i - 1, 0), 0)),
        out_shape=jax.ShapeDtypeStruct(x.shape, F32),
        scratch_shapes=[pltpu.VMEM((CONV_CH // LANES, 2 * ROW_TILE, LANES), F32),
                        pltpu.VMEM((CONV_CH // LANES, ROW_TILE, LANES), F32),
                        pltpu.VMEM((ROW_TILE, CONV_CH), BF16),
                        pltpu.VMEM((2, ROW_TILE, D_MODEL), F32),
                        pltpu.VMEM((ROW_TILE, D_MODEL), BF16),
                        pltpu.VMEM((2, ROW_TILE, MLP_CHUNK), BF16)],
        compiler_params=_params(1),
        name="layer0_tail",
    )(x, yb, g, w_z, conv_w, ln_g, ln_b, w_out, mlp_g, w_up, w_down)


class _mlp_stages:
    n_chunks = D_FF // MLP_CHUNK
    n_sub = MLP_CHUNK // MLP_SUB

    def __init__(self, x_ref, g_ref, wu_ref, wd_ref, final_g_ref, o_ref, h_ref, act_ref):
        self.x_ref, self.g_ref, self.wu_ref, self.wd_ref = x_ref, g_ref, wu_ref, wd_ref
        self.final_g_ref, self.o_ref, self.h_ref, self.act_ref = final_g_ref, o_ref, h_ref, act_ref

    def norm(self):
        self.h_ref[...] = _rmsnorm(self.x_ref[...], self.g_ref[...]).astype(BF16)

    def up(self, c, j):
        cols = slice(j * MLP_SUB, (j + 1) * MLP_SUB)
        a = _dot(self.h_ref[...], self.wu_ref[:, c * MLP_CHUNK + j * MLP_SUB:c * MLP_CHUNK + (j + 1) * MLP_SUB])
        self.act_ref[c % 2, :, cols] = jnp.square(jnp.maximum(a, 0.0)).astype(BF16)

    def down(self, c, j):
        cols = slice(j * MLP_SUB, (j + 1) * MLP_SUB)
        batch, t_blk, _ = self.o_ref.shape
        part = _dot(self.act_ref[c % 2], self.wd_ref[c * MLP_CHUNK:(c + 1) * MLP_CHUNK, cols])
        acc = self.x_ref[:, cols] if c == 0 else self.o_ref[:, :, cols].reshape(part.shape)
        self.o_ref[:, :, cols] = (acc + part).reshape(batch, t_blk, MLP_SUB)

    def finish(self):
        if self.final_g_ref is not None:
            shape = self.o_ref.shape
            acc = self.o_ref[...].reshape(shape[0] * shape[1], shape[2])
            self.o_ref[...] = _rmsnorm(acc, self.final_g_ref[...]).reshape(shape)


def _gelu_tanh(x):
    return 0.5 * x * (1.0 + jnp.tanh(0.7978845608028654 * (x + 0.044715 * (x * x * x))))


def _lru_kernel(x_ref, g_ref, win_ref, cw_ref, cb_ref, wri_ref, br_ref, bi_ref, lam_ref, wout_ref,
                mg_ref, wu_ref, wd_ref, fg_ref,
                o_ref, xbuf_ref, a_ref, b_ref, gate_ref, h_ref, y_ref, xmid_ref, hm_ref, act_ref):
    i = pl.program_id(0)
    batch, t_blk, _ = x_ref.shape
    rows = batch * t_blk
    halo = xbuf_ref.shape[1] - rows
    slot = i % 2

    @pl.when(i == 0)
    def _():
        xbuf_ref[:, 0:halo, :] = jnp.zeros((LRU_HEADS, halo, LANES), F32)
        h_ref[...] = jnp.zeros(h_ref.shape, F32)
        xmid_ref[1] = jnp.zeros((rows, D_MODEL), F32)

    @pl.when(i > 0)
    def _():
        xbuf_ref[:, 0:halo, :] = xbuf_ref[:, rows:rows + halo, :]

    x = x_ref[...].reshape(rows, D_MODEL)
    h = _rmsnorm(x, g_ref[...]).astype(BF16)
    neg_lam = -lam_ref[...]
    softplus = jnp.maximum(neg_lam, 0.0) + jnp.log1p(jnp.exp(-jnp.abs(neg_lam)))
    group = 2
    width = group * LRU_BLOCK

    def project_gate(p):
        cols = slice(p * width, (p + 1) * width)
        gate_ref[:, cols] = _gelu_tanh(_dot(h, win_ref[:, cols]))

    def project_x(p):
        xr = _dot(h, win_ref[:, D_MODEL + p * width:D_MODEL + (p + 1) * width])
        _to_time_major(xr, xbuf_ref, p * group, halo, batch, t_blk)

    def recurrence_inputs(hd):
        cols = slice(hd * LRU_BLOCK, (hd + 1) * LRU_BLOCK)
        xc = cb_ref[:, cols]
        for j in range(LRU_CONV_WIDTH):
            start = halo - (LRU_CONV_WIDTH - 1 - j) * batch
            xc = xc + xbuf_ref[hd, start:start + rows, :] * cw_ref[j:j + 1, cols]
        ri = _dot(xc.astype(BF16), wri_ref[hd])
        r = jax.nn.sigmoid(ri[:, 0:LRU_BLOCK] + br_ref[:, cols])
        ig = jax.nn.sigmoid(ri[:, LRU_BLOCK:2 * LRU_BLOCK] + bi_ref[:, cols])
        log_a = -LRU_C * r * softplus[:, cols]
        a = jnp.exp(log_a)
        a_ref[hd] = a
        b_ref[hd] = jnp.sqrt(-jnp.tanh(log_a) * (a * a + 1.0)) * (ig * xc)

    def scan(p):
        heads = range(p * group, (p + 1) * group)
        state = {hd: h_ref[hd] for hd in heads}
        for t in range(t_blk):
            step = slice(t * batch, (t + 1) * batch)
            for hd in heads:
                state[hd] = a_ref[hd, step, :] * state[hd] + b_ref[hd, step, :]
                b_ref[hd, step, :] = state[hd]
        for hd in heads:
            h_ref[hd] = state[hd]
            cols = slice(hd * LRU_BLOCK, (hd + 1) * LRU_BLOCK)
            for b in range(batch):
                blk = slice(b * t_blk, (b + 1) * t_blk)
                y_ref[blk, cols] = (_from_time_major(b_ref, hd, b, batch, t_blk) * gate_ref[blk, cols]).astype(BF16)

    def out_part(p):
        k_rows = slice(p * width, (p + 1) * width)
        return _dot(y_ref[:, k_rows], wout_ref[k_rows, :])

    n_groups = LRU_HEADS // group
    mlp = _mlp_stages(xmid_ref.at[1 - slot], mg_ref, wu_ref, wd_ref, fg_ref, o_ref, hm_ref, act_ref)
    assert n_groups == mlp.n_chunks and mlp.n_sub == 4 and group == 2
    out = [x]

    def add_out_part(p):
        out[0] = out[0] + out_part(p)

    mlp.norm()
    project_x(0)
    project_gate(0)
    for p in range(n_groups):
        last = p + 1 == n_groups
        mixer = [lambda: recurrence_inputs(p * group),
                 lambda: recurrence_inputs(p * group + 1),
                 (lambda: None) if last else (lambda: project_x(p + 1)),
                 (lambda: None) if last else (lambda: project_gate(p + 1)),
                 lambda: scan(p),
                 (lambda: add_out_part(p - 1)) if p > 0 else (lambda: None)]
        dense = [lambda j=j: mlp.up(p, j) for j in range(mlp.n_sub)]
        dense += [lambda j=j: mlp.down(p, j) for j in range(mlp.n_sub)]
        for k, dense_stage in enumerate(dense):
            dense_stage()
            if k < len(mixer):
                mixer[k]()
    xmid_ref[slot] = out[0] + out_part(n_groups - 1)
    mlp.finish()


def _layer1(x, g, w_in, conv_w, conv_b, w_ri, b_r, b_i, lam, w_out, mlp_g, w_up, w_down, final_g):
    batch, seq, _ = x.shape
    t_blk = ROW_TILE // batch
    n_blk = seq // t_blk
    halo = 64
    assert (LRU_CONV_WIDTH - 1) * batch <= halo and LRU_BLOCK == LANES
    vec = _const_spec((1, D_MODEL))
    slab = lambda n: pltpu.VMEM((LRU_HEADS, n, LANES), F32)
    return pl.pallas_call(
        _lru_kernel,
        grid=(n_blk + 1,),
        in_specs=[pl.BlockSpec((batch, t_blk, D_MODEL), lambda i: (0, jnp.minimum(i, n_blk - 1), 0)),
                  vec, _const_spec(w_in.shape), _const_spec(conv_w.shape), vec,
                  _const_spec(w_ri.shape), vec, vec, vec, _const_spec(w_out.shape),
                  vec, _const_spec(w_up.shape), _const_spec(w_down.shape), vec],
        out_specs=pl.BlockSpec((batch, t_blk, D_MODEL), lambda i: (0, jnp.maximum(i - 1, 0), 0)),
        out_shape=jax.ShapeDtypeStruct(x.shape, F32),
        scratch_shapes=[slab(halo + ROW_TILE), slab(ROW_TILE), slab(ROW_TILE),
                        pltpu.VMEM((ROW_TILE, D_MODEL), F32), slab(batch),
                        pltpu.VMEM((ROW_TILE, D_MODEL), BF16),
                        pltpu.VMEM((2, ROW_TILE, D_MODEL), F32),
                        pltpu.VMEM((ROW_TILE, D_MODEL), BF16),
                        pltpu.VMEM((2, ROW_TILE, MLP_CHUNK), BF16)],
        compiler_params=_params(1),
        name="layer1",
    )(x, g, w_in, conv_w, conv_b, w_ri, b_r, b_i, lam, w_out, mlp_g, w_up, w_down, final_g)


def kernel(x, mix_norm, mlp_norm, w_up, w_down, ab_w_in, ab_conv_w, ab_ln_g, ab_ln_b, ab_w_out,
           c_w_in, c_conv_w, c_conv_b, c_w_r, c_b_r, c_w_i, c_b_i, c_lambda, c_w_out, final_norm):
    batch, seq, d = x.shape
    vec = lambda a: a.reshape(1, -1).astype(F32)

    w_in = ab_w_in[0]
    o = 2 * CONV_CH
    w_q, w_k, w_v = (w_in[:, o + n * ATTN_WIDTH:o + (n + 1) * ATTN_WIDTH] for n in range(3))
    k, qt, vt = _qkv(x, vec(mix_norm[0]), w_k.astype(BF16), w_q.T.astype(BF16), w_v.T.astype(BF16))
    yb = _attention(qt, k, vt)
    conv_w = jnp.pad(ab_conv_w[0], ((0, 32 - CONV_WIDTH), (0, 0)))
    w_glu = w_in[:, 0:o].reshape(d, 2, CONV_CH // LANES, LANES).transpose(0, 2, 1, 3).reshape(d, o)
    x = _layer0_tail(x, yb, vec(mix_norm[0]), w_glu.astype(BF16), conv_w, vec(ab_ln_g[0]), vec(ab_ln_b[0]),
                     ab_w_out[0].astype(BF16), vec(mlp_norm[0]), w_up[0].astype(BF16), w_down[0].astype(BF16))

    w_ri = jnp.concatenate([c_w_r[0], c_w_i[0]], axis=-1).astype(BF16)
    lru_conv_w = jnp.pad(c_conv_w[0], ((0, SUBLANES - LRU_CONV_WIDTH), (0, 0)))
    return _layer1(x, vec(mix_norm[1]), c_w_in[0].astype(BF16), lru_conv_w, vec(c_conv_b[0]), w_ri,
                   vec(c_b_r[0]), vec(c_b_i[0]), vec(c_lambda[0]), c_w_out[0].astype(BF16),
                   vec(mlp_norm[1]), w_up[1].astype(BF16), w_down[1].astype(BF16), vec(final_norm))
```

```python
import functools

import jax
import jax.numpy as jnp
from jax import lax
from jax.experimental import pallas as pl
from jax.experimental.pallas import tpu as pltpu

F32 = jnp.float32
BF16 = jnp.bfloat16

EPS = 1e-6
D_MODEL = 1024
D_FF = 4 * D_MODEL
CONV_CH = 512
CONV_WIDTH = 31
ATTN_WIDTH = 512
HEAD_DIM = 64
N_HEADS = 8
MOBA_BLOCK = 256
MOBA_TOP_K = 3
LRU_HEADS = 8
LRU_BLOCK = 128
LRU_CONV_WIDTH = 4
LRU_C = 8.0

LANES = 128
SUBLANES = 8
BF16_ROWS = 16
HEADS_PER_STEP = LANES // HEAD_DIM
MASKED = -1e30
VMEM_LIMIT = 56 * 1024 * 1024

ROW_TILE = 512
QKV_ROWS = 2048
MLP_CHUNK = 1024
MLP_SUB = 256
CONV_ROWS = 128


def _dot(a, b):
    return jnp.dot(a, b, preferred_element_type=F32)


def _dot_nt(a, b):
    return lax.dot_general(a, b, (((1,), (1,)), ((), ())), preferred_element_type=F32)


def _rmsnorm(x, g):
    return x * lax.rsqrt(jnp.mean(x * x, axis=-1, keepdims=True) + EPS) * g


def _const_spec(shape):
    zeros = (0,) * len(shape)
    return pl.BlockSpec(shape, lambda *_: zeros, pipeline_mode=pl.Buffered(1))


def _params(n_axes):
    return pltpu.CompilerParams(dimension_semantics=("arbitrary",) * n_axes,
                                vmem_limit_bytes=VMEM_LIMIT)


def _qkv_kernel(x_ref, g_ref, wk_ref, wqt_ref, wvt_ref, k_ref, qt_ref, vt_ref):
    h = _rmsnorm(x_ref[...], g_ref[...]).astype(BF16)
    k_ref[...] = _dot(h, wk_ref[...]).astype(BF16)
    qt_ref[...] = _dot_nt(wqt_ref[...], h).astype(BF16)
    vt_ref[...] = _dot_nt(wvt_ref[...], h).astype(BF16)


def _qkv(x, g, wk, wqt, wvt):
    batch, seq, _ = x.shape
    w_spec = _const_spec((D_MODEL, ATTN_WIDTH))
    wt_spec = _const_spec((ATTN_WIDTH, D_MODEL))
    t_spec = pl.BlockSpec((None, ATTN_WIDTH, QKV_ROWS), lambda b, t: (b, 0, t))
    t_shape = jax.ShapeDtypeStruct((batch, ATTN_WIDTH, seq), BF16)
    return pl.pallas_call(
        _qkv_kernel,
        grid=(batch, seq // QKV_ROWS),
        in_specs=[pl.BlockSpec((None, QKV_ROWS, D_MODEL), lambda b, t: (b, t, 0)),
                  _const_spec((1, D_MODEL)), w_spec, wt_spec, wt_spec],
        out_specs=[pl.BlockSpec((None, QKV_ROWS, ATTN_WIDTH), lambda b, t: (b, t, 0)), t_spec, t_spec],
        out_shape=[jax.ShapeDtypeStruct((batch, seq, ATTN_WIDTH), BF16), t_shape, t_shape],
        compiler_params=_params(2),
        name="qkv_proj",
    )(x, g, wk, wqt, wvt)


def _head_lanes(hh):
    qk0 = hh * HEAD_DIM
    ex0 = (1 - hh) * HEAD_DIM
    return qk0, ex0, ex0 + BF16_ROWS


V_ROWS = HEAD_DIM + BF16_ROWS


def _attn_prepare(k_ref, vt_ref, slope_ref, kaug_ref, kbar_ref, vta_ref):
    n_keys = k_ref.shape[0]
    rowk = lax.broadcasted_iota(jnp.int32, (n_keys, LANES), 0)
    lane = lax.broadcasted_iota(jnp.int32, (n_keys, LANES), 1)
    tk = (rowk & (MOBA_BLOCK - 1)).astype(F32)
    blk = rowk >> (MOBA_BLOCK.bit_length() - 1)
    ones_rows = jnp.where(lax.broadcasted_iota(jnp.int32, (BF16_ROWS, n_keys), 0) == 0, 1.0, 0.0)
    for h in range(N_HEADS):
        pair, hh = divmod(h, HEADS_PER_STEP)
        qk0, ex0, bias0 = _head_lanes(hh)
        slope = slope_ref[h:h + 1, 0:LANES]
        extra = jnp.where(lane == ex0, 1.0,
                          jnp.where(lane == ex0 + 1, slope * tk,
                                    jnp.where(lane - bias0 == blk, 1.0, 0.0)))
        is_k = (lane >= qk0) & (lane < qk0 + HEAD_DIM)
        kaug_ref[h] = jnp.where(is_k, k_ref[:, pair * LANES:(pair + 1) * LANES], extra.astype(BF16))
        vta_ref[h, 0:HEAD_DIM, :] = vt_ref[h * HEAD_DIM:(h + 1) * HEAD_DIM, :]
        vta_ref[h, HEAD_DIM:V_ROWS, :] = ones_rows.astype(BF16)
    for j in range(n_keys // MOBA_BLOCK):
        kj = k_ref[j * MOBA_BLOCK:(j + 1) * MOBA_BLOCK, :].astype(F32)
        kbar_ref[j:j + 1, :] = jnp.sum(kj, axis=0, keepdims=True) * (1.0 / MOBA_BLOCK)


def _attn_block(c, qt_ref, slope_ref, kaug_ref, kbar_ref, vta_ref, o_ref):
    rowi = lax.broadcasted_iota(jnp.int32, (SUBLANES, MOBA_BLOCK), 0)
    tq = lax.broadcasted_iota(jnp.int32, (SUBLANES, MOBA_BLOCK), 1).astype(F32)
    causal = (lax.broadcasted_iota(jnp.int32, (MOBA_BLOCK, MOBA_BLOCK), 0)
              <= lax.broadcasted_iota(jnp.int32, (MOBA_BLOCK, MOBA_BLOCK), 1))
    zeros8 = jnp.zeros((SUBLANES, MOBA_BLOCK), F32)
    lane_k = lax.broadcasted_iota(jnp.int32, (SUBLANES, LANES), 1)
    n_past = c * MOBA_BLOCK
    own = slice(n_past, n_past + MOBA_BLOCK)

    qats = []
    for pair in range(N_HEADS // HEADS_PER_STEP):
        qt = qt_ref[pair * LANES:(pair + 1) * LANES, :]
        if c > 0:
            kbar = kbar_ref[:, pair * LANES:(pair + 1) * LANES]
            kbar = jnp.concatenate([jnp.where(lane_k < HEAD_DIM, kbar, 0.0),
                                    jnp.where(lane_k >= HEAD_DIM, kbar, 0.0)], axis=0)
            kb_hi = kbar.astype(BF16)
            kb_lo = (kbar - kb_hi.astype(F32)).astype(BF16)
            gates = _dot(kb_hi, qt) + _dot(kb_lo, qt)
        for hh in range(HEADS_PER_STEP):
            h = pair * HEADS_PER_STEP + hh
            qk0, ex0, bias0 = _head_lanes(hh)
            slope = slope_ref[h:h + 1, :]
            if c > 0:
                gate = gates[hh * SUBLANES:(hh + 1) * SUBLANES, :]
                elig = rowi < c
                sel = elig & (jnp.abs(gate) < jnp.inf)
                if c > MOBA_TOP_K:
                    beaten = zeros8
                    for jp in range(c):
                        gp = gate[jp:jp + 1, :]
                        beats = (gp > gate) | ((gp == gate) & (rowi > jp))
                        beaten = beaten + jnp.where(beats, 1.0, 0.0)
                    sel = sel & (beaten < float(MOBA_TOP_K))
                dist = (c - rowi).astype(F32) * float(MOBA_BLOCK)
                bias = jnp.where(elig, jnp.where(sel, -slope * dist, MASKED), 0.0)
            else:
                bias = zeros8
            ext_a = jnp.where(rowi == 0, -slope * tq, jnp.where(rowi == 1, 1.0, 0.0))
            ext = jnp.concatenate([ext_a, zeros8, bias, zeros8], axis=0).astype(BF16)
            pad = jnp.zeros((HEAD_DIM - ext.shape[0], MOBA_BLOCK), BF16)
            qs = qt[qk0:qk0 + HEAD_DIM, :] * jnp.asarray(HEAD_DIM ** -0.5, BF16)
            qats.append(jnp.concatenate([qs, ext, pad] if hh == 0 else [ext, pad, qs], axis=0))

    def logits(h):
        s_own = jnp.where(causal, _dot(kaug_ref[h, own, :], qats[h]), -jnp.inf)
        m = jnp.max(s_own, axis=0, keepdims=True)
        s_past = None
        if c > 0:
            s_past = _dot(kaug_ref[h, 0:n_past, :], qats[h])
            m = jnp.maximum(m, jnp.max(s_past, axis=0, keepdims=True))
        return s_own, s_past, m

    def weighted_values(h, s_own, s_past, m):
        acc = _dot(vta_ref[h, :, own], jnp.exp(s_own - m).astype(BF16))
        if c > 0:
            acc = acc + _dot(vta_ref[h, :, 0:n_past], jnp.exp(s_past - m).astype(BF16))
        return acc[0:HEAD_DIM, :] * (1.0 / acc[HEAD_DIM:HEAD_DIM + 1, :])

    outs = []
    state = logits(0)
    for h in range(N_HEADS):
        nxt = logits(h + 1) if h + 1 < N_HEADS else None
        outs.append(weighted_values(h, *state))
        state = nxt
    for pair in range(N_HEADS // HEADS_PER_STEP):
        o_pair = jnp.concatenate(outs[pair * HEADS_PER_STEP:(pair + 1) * HEADS_PER_STEP], axis=0)
        o_ref[:, pair * LANES:(pair + 1) * LANES] = o_pair.T.astype(o_ref.dtype)


def _attn_kernel(slope_ref, qt_ref, k_ref, vt_ref, o_ref, kaug_ref, kbar_ref, vta_ref):
    _attn_prepare(k_ref, vt_ref, slope_ref, kaug_ref, kbar_ref, vta_ref)
    for c in range(k_ref.shape[0] // MOBA_BLOCK):
        blk = slice(c * MOBA_BLOCK, (c + 1) * MOBA_BLOCK)
        _attn_block(c, qt_ref.at[:, blk], slope_ref, kaug_ref, kbar_ref, vta_ref, o_ref.at[blk])


def _attention(qt, k, vt):
    batch, seq, _ = k.shape
    n_blk = seq // MOBA_BLOCK
    slopes = 2.0 ** (-8.0 * jnp.arange(1, N_HEADS + 1, dtype=F32) / N_HEADS)
    slopes = jnp.broadcast_to(slopes.reshape(N_HEADS, 1), (N_HEADS, MOBA_BLOCK))
    return pl.pallas_call(
        _attn_kernel,
        grid=(batch,),
        in_specs=[
            _const_spec((N_HEADS, MOBA_BLOCK)),
            pl.BlockSpec((None, ATTN_WIDTH, seq), lambda b: (b, 0, 0)),
            pl.BlockSpec((None, seq, ATTN_WIDTH), lambda b: (b, 0, 0)),
            pl.BlockSpec((None, ATTN_WIDTH, seq), lambda b: (b, 0, 0)),
        ],
        out_specs=pl.BlockSpec((None, seq, ATTN_WIDTH), lambda b: (b, 0, 0)),
        out_shape=jax.ShapeDtypeStruct((batch, seq, ATTN_WIDTH), BF16),
        scratch_shapes=[pltpu.VMEM((N_HEADS, seq, LANES), BF16),
                        pltpu.VMEM((n_blk, ATTN_WIDTH), F32),
                        pltpu.VMEM((N_HEADS, V_ROWS, seq), BF16)],
        compiler_params=_params(1),
        name="moba_attention",
    )(slopes, qt, k, vt)


def _to_time_major(val, buf_ref, slab0, row0, batch, t_blk):
    for s in range(val.shape[1] // LANES):
        for b in range(batch):
            buf_ref[slab0 + s, pl.ds(row0 + b, t_blk, stride=batch), :] = (
                val[b * t_blk:(b + 1) * t_blk, s * LANES:(s + 1) * LANES])


def _from_time_major(buf_ref, s, b, batch, t_blk):
    return buf_ref[s, pl.ds(b, t_blk, stride=batch), :]


def _mix0_kernel(x_ref, yb_ref, g_ref, wz_ref, cw_ref, lg_ref, lb_ref, w_ref, mg_ref, wu_ref, wd_ref,
                 o_ref, zbuf_ref, conv_ref, lhs_ref, xmid_ref, hm_ref, act_ref):
    i = pl.program_id(0)
    batch, t_blk, _ = x_ref.shape
    rows = batch * t_blk
    n_slab = CONV_CH // LANES
    slot = i % 2

    @pl.when(i == 0)
    def _():
        zbuf_ref[:, 0:rows, :] = jnp.zeros((n_slab, rows, LANES), F32)
        xmid_ref[1] = jnp.zeros((rows, D_MODEL), F32)

    @pl.when(i > 0)
    def _():
        zbuf_ref[:, 0:rows, :] = zbuf_ref[:, rows:2 * rows, :]

    x = x_ref[...].reshape(rows, D_MODEL)
    h = _rmsnorm(x, g_ref[...]).astype(BF16)

    glu_in = {}
    stats = {}
    mine = xmid_ref.at[slot]

    def project(s):
        glu_in[s] = _dot(h, wz_ref[:, 2 * s * LANES:2 * (s + 1) * LANES])

    tokens = {}

    def conv(s):
        u = glu_in.pop(s)
        z = u[:, 0:LANES] * jax.nn.sigmoid(u[:, LANES:2 * LANES])
        _to_time_major(z, zbuf_ref, s, rows, batch, t_blk)
        for r0 in range(0, rows, CONV_ROWS):
            q = s * (rows // CONV_ROWS) + r0 // CONV_ROWS
            tok = tokens[1 + (5 * q) // 4]
            zero = jnp.where(jnp.abs(tok) < jnp.inf, tok - tok, 0.0)
            acc = jnp.concatenate([zero] * (CONV_ROWS // SUBLANES), axis=0)
            for j in range(CONV_WIDTH):
                start = r0 + rows - (CONV_WIDTH - 1 - j) * batch
                acc = acc + zbuf_ref[s, start:start + CONV_ROWS, :] * cw_ref[j:j + 1, s * LANES:(s + 1) * LANES]
            conv_ref[s, r0:r0 + CONV_ROWS, :] = acc

    def attention_part(j):
        cols = slice(j * CONV_CH, (j + 1) * CONV_CH)
        yb = yb_ref[...].reshape(rows, ATTN_WIDTH)
        mine[:, cols] = x_ref[:, :, cols].reshape(rows, CONV_CH) + _dot(yb, w_ref[CONV_CH:CONV_CH + ATTN_WIDTH, cols])

    def layernorm_stats():
        y = [conv_ref[s] for s in range(n_slab)]
        mu = jnp.sum(y[0] + y[1] + y[2] + y[3], axis=-1, keepdims=True) * (1.0 / CONV_CH)
        d = [ys - mu for ys in y]
        var = jnp.sum(d[0] * d[0] + d[1] * d[1] + d[2] * d[2] + d[3] * d[3], axis=-1, keepdims=True) * (1.0 / CONV_CH)
        stats["mu"], stats["inv"] = mu, lax.rsqrt(var + EPS)

    def normalize(s):
        cols = slice(s * LANES, (s + 1) * LANES)
        yn = (conv_ref[s] - stats["mu"]) * stats["inv"] * lg_ref[:, cols] + lb_ref[:, cols]
        conv_ref[s] = yn * jax.nn.sigmoid(yn)
        for b in range(batch):
            lhs_ref[b * t_blk:(b + 1) * t_blk, cols] = _from_time_major(conv_ref, s, b, batch, t_blk).astype(BF16)

    def conv_part(first):
        k_rows = slice(first * LANES, (first + 2) * LANES)
        mine[...] = mine[...] + _dot(lhs_ref[:, k_rows], w_ref[k_rows, :])

    mlp = _mlp_stages(xmid_ref.at[1 - slot], mg_ref, wu_ref, wd_ref, None, o_ref, hm_ref, act_ref)
    dense = [functools.partial(stage, c, j) for c in range(mlp.n_chunks)
             for stage in (mlp.up, mlp.down) for j in range(mlp.n_sub)]
    mixer_after = {0: lambda: project(1), 5: lambda: conv(0), 3: lambda: project(2), 10: lambda: conv(1),
                   7: lambda: project(3), 15: lambda: conv(2), 11: lambda: attention_part(0),
                   20: lambda: conv(3), 13: lambda: attention_part(1), 21: layernorm_stats,
                   22: lambda: normalize(0), 23: lambda: normalize(1), 24: lambda: conv_part(0),
                   25: lambda: normalize(2), 26: lambda: normalize(3), 27: lambda: conv_part(2)}
    assert max(mixer_after) < len(dense)
    mlp.norm()
    project(0)
    for k, dense_stage in enumerate(dense):
        tokens[k] = dense_stage()
        if k in mixer_after:
            mixer_after[k]()


def _layer0_tail(x, yb, g, w_z, conv_w, ln_g, ln_b, w_out, mlp_g, w_up, w_down):
    batch, seq, _ = x.shape
    t_blk = ROW_TILE // batch
    n_blk = seq // t_blk
    assert (CONV_WIDTH - 1) * batch <= ROW_TILE and CONV_CH == 4 * LANES
    blk = lambda width: pl.BlockSpec((batch, t_blk, width), lambda i: (0, jnp.minimum(i, n_blk - 1), 0))
    vec = _const_spec((1, CONV_CH))
    return pl.pallas_call(
        _mix0_kernel,
        grid=(n_blk + 1,),
        in_specs=[blk(D_MODEL), blk(ATTN_WIDTH), _const_spec((1, D_MODEL)), _const_spec(w_z.shape),
                  _const_spec(conv_w.shape), vec, vec, _const_spec(w_out.shape),
                  _const_spec((1, D_MODEL)), _const_spec(w_up.shape), _const_spec(w_down.shape)],
        out_specs=pl.BlockSpec((batch, t_blk, D_MODEL), lambda i: (0, jnp.maximum(i - 1, 0), 0)),
        out_shape=jax.ShapeDtypeStruct(x.shape, F32),
        scratch_shapes=[pltpu.VMEM((CONV_CH // LANES, 2 * ROW_TILE, LANES), F32),
                        pltpu.VMEM((CONV_CH // LANES, ROW_TILE, LANES), F32),
                        pltpu.VMEM((ROW_TILE, CONV_CH), BF16),
                        pltpu.VMEM((2, ROW_TILE, D_MODEL), F32),
                        pltpu.VMEM((ROW_TILE, D_MODEL), BF16),
                        pltpu.VMEM((2, ROW_TILE, MLP_CHUNK), BF16)],
        compiler_params=_params(1),
        name="layer0_tail",
    )(x, yb, g, w_z, conv_w, ln_g, ln_b, w_out, mlp_g, w_up, w_down)


class _mlp_stages:
    n_chunks = D_FF // MLP_CHUNK
    n_sub = MLP_CHUNK // MLP_SUB

    def __init__(self, x_ref, g_ref, wu_ref, wd_ref, final_g_ref, o_ref, h_ref, act_ref):
        self.x_ref, self.g_ref, self.wu_ref, self.wd_ref = x_ref, g_ref, wu_ref, wd_ref
        self.final_g_ref, self.o_ref, self.h_ref, self.act_ref = final_g_ref, o_ref, h_ref, act_ref

    def norm(self):
        self.h_ref[...] = _rmsnorm(self.x_ref[...], self.g_ref[...]).astype(BF16)

    def up(self, c, j):
        cols = slice(j * MLP_SUB, (j + 1) * MLP_SUB)
        a = _dot(self.h_ref[...], self.wu_ref[:, c * MLP_CHUNK + j * MLP_SUB:c * MLP_CHUNK + (j + 1) * MLP_SUB])
        self.act_ref[c % 2, :, cols] = jnp.square(jnp.maximum(a, 0.0)).astype(BF16)
        return self.act_ref[c % 2, 0:BF16_ROWS, j * MLP_SUB:j * MLP_SUB + LANES].astype(F32)[0:SUBLANES]

    def down(self, c, j):
        cols = slice(j * MLP_SUB, (j + 1) * MLP_SUB)
        batch, t_blk, _ = self.o_ref.shape
        part = _dot(self.act_ref[c % 2], self.wd_ref[c * MLP_CHUNK:(c + 1) * MLP_CHUNK, cols])
        acc = self.x_ref[:, cols] if c == 0 else self.o_ref[:, :, cols].reshape(part.shape)
        self.o_ref[:, :, cols] = (acc + part).reshape(batch, t_blk, MLP_SUB)
        return self.o_ref[0, 0:SUBLANES, j * MLP_SUB:j * MLP_SUB + LANES]

    def finish(self):
        if self.final_g_ref is not None:
            shape = self.o_ref.shape
            acc = self.o_ref[...].reshape(shape[0] * shape[1], shape[2])
            self.o_ref[...] = _rmsnorm(acc, self.final_g_ref[...]).reshape(shape)


def _gelu_tanh(x):
    return 0.5 * x * (1.0 + jnp.tanh(0.7978845608028654 * (x + 0.044715 * (x * x * x))))


def _lru_kernel(x_ref, g_ref, win_ref, cw_ref, cb_ref, wri_ref, br_ref, bi_ref, lam_ref, wout_ref,
                mg_ref, wu_ref, wd_ref, fg_ref,
                o_ref, xbuf_ref, a_ref, b_ref, gate_ref, h_ref, y_ref, xmid_ref, hm_ref, act_ref):
    i = pl.program_id(0)
    batch, t_blk, _ = x_ref.shape
    rows = batch * t_blk
    halo = xbuf_ref.shape[1] - rows
    slot = i % 2

    @pl.when(i == 0)
    def _():
        xbuf_ref[:, 0:halo, :] = jnp.zeros((LRU_HEADS, halo, LANES), F32)
        h_ref[...] = jnp.zeros(h_ref.shape, F32)
        xmid_ref[1] = jnp.zeros((rows, D_MODEL), F32)

    @pl.when(i > 0)
    def _():
        xbuf_ref[:, 0:halo, :] = xbuf_ref[:, rows:rows + halo, :]

    x = x_ref[...].reshape(rows, D_MODEL)
    h = _rmsnorm(x, g_ref[...]).astype(BF16)
    neg_lam = -lam_ref[...]
    softplus = jnp.maximum(neg_lam, 0.0) + jnp.log1p(jnp.exp(-jnp.abs(neg_lam)))
    group = 2
    width = group * LRU_BLOCK

    def project_gate(p):
        cols = slice(p * width, (p + 1) * width)
        gate_ref[:, cols] = _gelu_tanh(_dot(h, win_ref[:, cols]))

    def project_x(p):
        xr = _dot(h, win_ref[:, D_MODEL + p * width:D_MODEL + (p + 1) * width])
        _to_time_major(xr, xbuf_ref, p * group, halo, batch, t_blk)

    def recurrence_inputs(hd):
        cols = slice(hd * LRU_BLOCK, (hd + 1) * LRU_BLOCK)
        xc = cb_ref[:, cols]
        for j in range(LRU_CONV_WIDTH):
            start = halo - (LRU_CONV_WIDTH - 1 - j) * batch
            xc = xc + xbuf_ref[hd, start:start + rows, :] * cw_ref[j:j + 1, cols]
        ri = _dot(xc.astype(BF16), wri_ref[hd])
        r = jax.nn.sigmoid(ri[:, 0:LRU_BLOCK] + br_ref[:, cols])
        ig = jax.nn.sigmoid(ri[:, LRU_BLOCK:2 * LRU_BLOCK] + bi_ref[:, cols])
        log_a = -LRU_C * r * softplus[:, cols]
        a = jnp.exp(log_a)
        a_ref[hd] = a
        b_ref[hd] = jnp.sqrt(-jnp.tanh(log_a) * (a * a + 1.0)) * (ig * xc)

    def scan(p):
        heads = range(p * group, (p + 1) * group)
        state = {hd: h_ref[hd] for hd in heads}
        for t in range(t_blk):
            step = slice(t * batch, (t + 1) * batch)
            for hd in heads:
                state[hd] = a_ref[hd, step, :] * state[hd] + b_ref[hd, step, :]
                b_ref[hd, step, :] = state[hd]
        for hd in heads:
            h_ref[hd] = state[hd]
            cols = slice(hd * LRU_BLOCK, (hd + 1) * LRU_BLOCK)
            for b in range(batch):
                blk = slice(b * t_blk, (b + 1) * t_blk)
                y_ref[blk, cols] = (_from_time_major(b_ref, hd, b, batch, t_blk) * gate_ref[blk, cols]).astype(BF16)

    def out_part(p):
        k_rows = slice(p * width, (p + 1) * width)
        return _dot(y_ref[:, k_rows], wout_ref[k_rows, :])

    n_groups = LRU_HEADS // group
    mlp = _mlp_stages(xmid_ref.at[1 - slot], mg_ref, wu_ref, wd_ref, fg_ref, o_ref, hm_ref, act_ref)
    assert n_groups == mlp.n_chunks and mlp.n_sub == 4 and group == 2
    out = [x]

    def add_out_part(p):
        out[0] = out[0] + out_part(p)

    mlp.norm()
    project_x(0)
    project_gate(0)
    for p in range(n_groups):
        last = p + 1 == n_groups
        mixer = [lambda: recurrence_inputs(p * group),
                 lambda: recurrence_inputs(p * group + 1),
                 (lambda: None) if last else (lambda: project_x(p + 1)),
                 (lambda: None) if last else (lambda: project_gate(p + 1)),
                 lambda: scan(p),
                 (lambda: add_out_part(p - 1)) if p > 0 else (lambda: None)]
        dense = [lambda j=j: mlp.up(p, j) for j in range(mlp.n_sub)]
        dense += [lambda j=j: mlp.down(p, j) for j in range(mlp.n_sub)]
        for k, dense_stage in enumerate(dense):
            dense_stage()
            if k < len(mixer):
                mixer[k]()
    xmid_ref[slot] = out[0] + out_part(n_groups - 1)
    mlp.finish()


def _layer1(x, g, w_in, conv_w, conv_b, w_ri, b_r, b_i, lam, w_out, mlp_g, w_up, w_down, final_g):
    batch, seq, _ = x.shape
    t_blk = ROW_TILE // batch
    n_blk = seq // t_blk
    halo = 64
    assert (LRU_CONV_WIDTH - 1) * batch <= halo and LRU_BLOCK == LANES
    vec = _const_spec((1, D_MODEL))
    slab = lambda n: pltpu.VMEM((LRU_HEADS, n, LANES), F32)
    return pl.pallas_call(
        _lru_kernel,
        grid=(n_blk + 1,),
        in_specs=[pl.BlockSpec((batch, t_blk, D_MODEL), lambda i: (0, jnp.minimum(i, n_blk - 1), 0)),
                  vec, _const_spec(w_in.shape), _const_spec(conv_w.shape), vec,
                  _const_spec(w_ri.shape), vec, vec, vec, _const_spec(w_out.shape),
                  vec, _const_spec(w_up.shape), _const_spec(w_down.shape), vec],
        out_specs=pl.BlockSpec((batch, t_blk, D_MODEL), lambda i: (0, jnp.maximum(i - 1, 0), 0)),
        out_shape=jax.ShapeDtypeStruct(x.shape, F32),
        scratch_shapes=[slab(halo + ROW_TILE), slab(ROW_TILE), slab(ROW_TILE),
                        pltpu.VMEM((ROW_TILE, D_MODEL), F32), slab(batch),
                        pltpu.VMEM((ROW_TILE, D_MODEL), BF16),
                        pltpu.VMEM((2, ROW_TILE, D_MODEL), F32),
                        pltpu.VMEM((ROW_TILE, D_MODEL), BF16),
                        pltpu.VMEM((2, ROW_TILE, MLP_CHUNK), BF16)],
        compiler_params=_params(1),
        name="layer1",
    )(x, g, w_in, conv_w, conv_b, w_ri, b_r, b_i, lam, w_out, mlp_g, w_up, w_down, final_g)


def kernel(x, mix_norm, mlp_norm, w_up, w_down, ab_w_in, ab_conv_w, ab_ln_g, ab_ln_b, ab_w_out,
           c_w_in, c_conv_w, c_conv_b, c_w_r, c_b_r, c_w_i, c_b_i, c_lambda, c_w_out, final_norm):
    batch, seq, d = x.shape
    vec = lambda a: a.reshape(1, -1).astype(F32)

    w_in = ab_w_in[0]
    o = 2 * CONV_CH
    w_q, w_k, w_v = (w_in[:, o + n * ATTN_WIDTH:o + (n + 1) * ATTN_WIDTH] for n in range(3))
    k, qt, vt = _qkv(x, vec(mix_norm[0]), w_k.astype(BF16), w_q.T.astype(BF16), w_v.T.astype(BF16))
    yb = _attention(qt, k, vt)
    conv_w = jnp.pad(ab_conv_w[0], ((0, 32 - CONV_WIDTH), (0, 0)))
    w_glu = w_in[:, 0:o].reshape(d, 2, CONV_CH // LANES, LANES).transpose(0, 2, 1, 3).reshape(d, o)
    x = _layer0_tail(x, yb, vec(mix_norm[0]), w_glu.astype(BF16), conv_w, vec(ab_ln_g[0]), vec(ab_ln_b[0]),
                     ab_w_out[0].astype(BF16), vec(mlp_norm[0]), w_up[0].astype(BF16), w_down[0].astype(BF16))

    w_ri = jnp.concatenate([c_w_r[0], c_w_i[0]], axis=-1).astype(BF16)
    lru_conv_w = jnp.pad(c_conv_w[0], ((0, SUBLANES - LRU_CONV_WIDTH), (0, 0)))
    return _layer1(x, vec(mix_norm[1]), c_w_in[0].astype(BF16), lru_conv_w, vec(c_conv_b[0]), w_ri,
                   vec(c_b_r[0]), vec(c_b_i[0]), vec(c_lambda[0]), c_w_out[0].astype(BF16),
                   vec(mlp_norm[1]), w_up[1].astype(BF16), w_down[1].astype(BF16), vec(final_norm))
```

```python
import functools

import jax
import jax.numpy as jnp
from jax import lax
from jax.experimental import pallas as pl
from jax.experimental.pallas import tpu as pltpu

F32 = jnp.float32
BF16 = jnp.bfloat16

EPS = 1e-6
D_MODEL = 1024
D_FF = 4 * D_MODEL
CONV_CH = 512
CONV_WIDTH = 31
ATTN_WIDTH = 512
HEAD_DIM = 64
N_HEADS = 8
MOBA_BLOCK = 256
MOBA_TOP_K = 3
LRU_HEADS = 8
LRU_BLOCK = 128
LRU_CONV_WIDTH = 4
LRU_C = 8.0

LANES = 128
SUBLANES = 8
BF16_ROWS = 16
HEADS_PER_STEP = LANES // HEAD_DIM
MASKED = -3e38
VMEM_LIMIT = 56 * 1024 * 1024

ROW_TILE = 512
QKV_ROWS = 2048
MLP_CHUNK = 1024
MLP_SUB = 256
CONV_ROWS = 128


def _dot(a, b):
    return jnp.dot(a, b, preferred_element_type=F32)


def _dot_nt(a, b):
    return lax.dot_general(a, b, (((1,), (1,)), ((), ())), preferred_element_type=F32)


def _rmsnorm(x, g):
    return x * lax.rsqrt(jnp.mean(x * x, axis=-1, keepdims=True) + EPS) * g


def _const_spec(shape):
    zeros = (0,) * len(shape)
    return pl.BlockSpec(shape, lambda *_: zeros, pipeline_mode=pl.Buffered(1))


def _params(n_axes):
    return pltpu.CompilerParams(dimension_semantics=("arbitrary",) * n_axes,
                                vmem_limit_bytes=VMEM_LIMIT)


def _qkv_kernel(x_ref, g_ref, wk_ref, wqt_ref, wvt_ref, k_ref, qt_ref, vt_ref):
    h = _rmsnorm(x_ref[...], g_ref[...]).astype(BF16)
    k_ref[...] = _dot(h, wk_ref[...]).astype(BF16)
    qt_ref[...] = _dot_nt(wqt_ref[...], h).astype(BF16)
    vt_ref[...] = _dot_nt(wvt_ref[...], h).astype(BF16)


def _qkv(x, g, wk, wqt, wvt):
    batch, seq, _ = x.shape
    w_spec = _const_spec((D_MODEL, ATTN_WIDTH))
    wt_spec = _const_spec((ATTN_WIDTH, D_MODEL))
    t_spec = pl.BlockSpec((None, ATTN_WIDTH, QKV_ROWS), lambda b, t: (b, 0, t))
    t_shape = jax.ShapeDtypeStruct((batch, ATTN_WIDTH, seq), BF16)
    return pl.pallas_call(
        _qkv_kernel,
        grid=(batch, seq // QKV_ROWS),
        in_specs=[pl.BlockSpec((None, QKV_ROWS, D_MODEL), lambda b, t: (b, t, 0)),
                  _const_spec((1, D_MODEL)), w_spec, wt_spec, wt_spec],
        out_specs=[pl.BlockSpec((None, QKV_ROWS, ATTN_WIDTH), lambda b, t: (b, t, 0)), t_spec, t_spec],
        out_shape=[jax.ShapeDtypeStruct((batch, seq, ATTN_WIDTH), BF16), t_shape, t_shape],
        compiler_params=_params(2),
        name="qkv_proj",
    )(x, g, wk, wqt, wvt)


def _head_lanes(hh):
    qk0 = hh * HEAD_DIM
    ex0 = (1 - hh) * HEAD_DIM
    return qk0, ex0, ex0 + BF16_ROWS


V_ROWS = HEAD_DIM + BF16_ROWS


def _attn_prepare(k_ref, vt_ref, slope_ref, kaug_ref, kbar_ref, vta_ref):
    n_keys = k_ref.shape[0]
    rowk = lax.broadcasted_iota(jnp.int32, (n_keys, LANES), 0)
    lane = lax.broadcasted_iota(jnp.int32, (n_keys, LANES), 1)
    tk = (rowk & (MOBA_BLOCK - 1)).astype(F32)
    blk = rowk >> (MOBA_BLOCK.bit_length() - 1)
    ones_rows = jnp.where(lax.broadcasted_iota(jnp.int32, (BF16_ROWS, n_keys), 0) == 0, 1.0, 0.0)
    for h in range(N_HEADS):
        pair, hh = divmod(h, HEADS_PER_STEP)
        qk0, ex0, bias0 = _head_lanes(hh)
        slope = slope_ref[h:h + 1, 0:LANES]
        extra = jnp.where(lane == ex0, 1.0,
                          jnp.where(lane == ex0 + 1, slope * tk,
                                    jnp.where(lane - bias0 == blk, 1.0, 0.0)))
        is_k = (lane >= qk0) & (lane < qk0 + HEAD_DIM)
        kaug_ref[h] = jnp.where(is_k, k_ref[:, pair * LANES:(pair + 1) * LANES], extra.astype(BF16))
        vta_ref[h, 0:HEAD_DIM, :] = vt_ref[h * HEAD_DIM:(h + 1) * HEAD_DIM, :]
        vta_ref[h, HEAD_DIM:V_ROWS, :] = ones_rows.astype(BF16)
    for j in range(n_keys // MOBA_BLOCK):
        kj = k_ref[j * MOBA_BLOCK:(j + 1) * MOBA_BLOCK, :].astype(F32)
        kbar_ref[j:j + 1, :] = jnp.sum(kj, axis=0, keepdims=True) * (1.0 / MOBA_BLOCK)


def _attn_block(c, qt_ref, slope_ref, kaug_ref, kbar_ref, vta_ref, o_ref):
    rowi = lax.broadcasted_iota(jnp.int32, (SUBLANES, MOBA_BLOCK), 0)
    tq = lax.broadcasted_iota(jnp.int32, (SUBLANES, MOBA_BLOCK), 1).astype(F32)
    causal = (lax.broadcasted_iota(jnp.int32, (MOBA_BLOCK, MOBA_BLOCK), 0)
              <= lax.broadcasted_iota(jnp.int32, (MOBA_BLOCK, MOBA_BLOCK), 1))
    zeros8 = jnp.zeros((SUBLANES, MOBA_BLOCK), F32)
    lane_k = lax.broadcasted_iota(jnp.int32, (SUBLANES, LANES), 1)
    n_past = c * MOBA_BLOCK
    own = slice(n_past, n_past + MOBA_BLOCK)

    qats = []
    for pair in range(N_HEADS // HEADS_PER_STEP):
        qt = qt_ref[pair * LANES:(pair + 1) * LANES, :]
        if c > 0:
            kbar = kbar_ref[:, pair * LANES:(pair + 1) * LANES]
            kbar = jnp.concatenate([jnp.where(lane_k < HEAD_DIM, kbar, 0.0),
                                    jnp.where(lane_k >= HEAD_DIM, kbar, 0.0)], axis=0)
            kb_hi = kbar.astype(BF16)
            kb_lo = (kbar - kb_hi.astype(F32)).astype(BF16)
            gates = _dot(kb_hi, qt) + _dot(kb_lo, qt)
        for hh in range(HEADS_PER_STEP):
            h = pair * HEADS_PER_STEP + hh
            qk0, ex0, bias0 = _head_lanes(hh)
            slope = slope_ref[h:h + 1, :]
            if c > 0:
                gate = gates[hh * SUBLANES:(hh + 1) * SUBLANES, :]
                elig = rowi < c
                sel = elig & (jnp.abs(gate) < jnp.inf)
                if c > MOBA_TOP_K:
                    beaten = zeros8
                    for jp in range(c):
                        gp = gate[jp:jp + 1, :]
                        beats = (gp > gate) | ((gp == gate) & (rowi > jp))
                        beaten = beaten + jnp.where(beats, 1.0, 0.0)
                    sel = sel & (beaten < float(MOBA_TOP_K))
                dist = (c - rowi).astype(F32) * float(MOBA_BLOCK)
                bias = jnp.where(elig, jnp.where(sel, -slope * dist, MASKED), 0.0)
            else:
                bias = zeros8
            ext_a = jnp.where(rowi == 0, -slope * tq, jnp.where(rowi == 1, 1.0, 0.0))
            ext = jnp.concatenate([ext_a, zeros8, bias, zeros8], axis=0).astype(BF16)
            pad = jnp.zeros((HEAD_DIM - ext.shape[0], MOBA_BLOCK), BF16)
            qs = qt[qk0:qk0 + HEAD_DIM, :] * jnp.asarray(HEAD_DIM ** -0.5, BF16)
            qats.append(jnp.concatenate([qs, ext, pad] if hh == 0 else [ext, pad, qs], axis=0))

    def logits(h):
        s_own = jnp.where(causal, _dot(kaug_ref[h, own, :], qats[h]), -jnp.inf)
        m = jnp.max(s_own, axis=0, keepdims=True)
        s_past = None
        if c > 0:
            s_past = _dot(kaug_ref[h, 0:n_past, :], qats[h])
            m = jnp.maximum(m, jnp.max(s_past, axis=0, keepdims=True))
        return s_own, s_past, m

    def weighted_values(h, s_own, s_past, m):
        acc = _dot(vta_ref[h, :, own], jnp.exp((s_own - m).astype(BF16)))
        if c > 0:
            acc = acc + _dot(vta_ref[h, :, 0:n_past], jnp.exp((s_past - m).astype(BF16)))
        return acc[0:HEAD_DIM, :] * (1.0 / acc[HEAD_DIM:HEAD_DIM + 1, :])

    outs = []
    state = logits(0)
    for h in range(N_HEADS):
        nxt = logits(h + 1) if h + 1 < N_HEADS else None
        outs.append(weighted_values(h, *state))
        state = nxt
    for pair in range(N_HEADS // HEADS_PER_STEP):
        o_pair = jnp.concatenate(outs[pair * HEADS_PER_STEP:(pair + 1) * HEADS_PER_STEP], axis=0)
        o_ref[:, pair * LANES:(pair + 1) * LANES] = o_pair.T.astype(o_ref.dtype)


def _attn_kernel(slope_ref, qt_ref, k_ref, vt_ref, o_ref, kaug_ref, kbar_ref, vta_ref):
    _attn_prepare(k_ref, vt_ref, slope_ref, kaug_ref, kbar_ref, vta_ref)
    for c in range(k_ref.shape[0] // MOBA_BLOCK):
        blk = slice(c * MOBA_BLOCK, (c + 1) * MOBA_BLOCK)
        _attn_block(c, qt_ref.at[:, blk], slope_ref, kaug_ref, kbar_ref, vta_ref, o_ref.at[blk])


def _attention(qt, k, vt):
    batch, seq, _ = k.shape
    n_blk = seq // MOBA_BLOCK
    slopes = 2.0 ** (-8.0 * jnp.arange(1, N_HEADS + 1, dtype=F32) / N_HEADS)
    slopes = jnp.broadcast_to(slopes.reshape(N_HEADS, 1), (N_HEADS, MOBA_BLOCK))
    return pl.pallas_call(
        _attn_kernel,
        grid=(batch,),
        in_specs=[
            _const_spec((N_HEADS, MOBA_BLOCK)),
            pl.BlockSpec((None, ATTN_WIDTH, seq), lambda b: (b, 0, 0)),
            pl.BlockSpec((None, seq, ATTN_WIDTH), lambda b: (b, 0, 0)),
            pl.BlockSpec((None, ATTN_WIDTH, seq), lambda b: (b, 0, 0)),
        ],
        out_specs=pl.BlockSpec((None, seq, ATTN_WIDTH), lambda b: (b, 0, 0)),
        out_shape=jax.ShapeDtypeStruct((batch, seq, ATTN_WIDTH), BF16),
        scratch_shapes=[pltpu.VMEM((N_HEADS, seq, LANES), BF16),
                        pltpu.VMEM((n_blk, ATTN_WIDTH), F32),
                        pltpu.VMEM((N_HEADS, V_ROWS, seq), BF16)],
        compiler_params=_params(1),
        name="moba_attention",
    )(slopes, qt, k, vt)


def _to_time_major(val, buf_ref, slab0, row0, batch, t_blk):
    for s in range(val.shape[1] // LANES):
        for b in range(batch):
            buf_ref[slab0 + s, pl.ds(row0 + b, t_blk, stride=batch), :] = (
                val[b * t_blk:(b + 1) * t_blk, s * LANES:(s + 1) * LANES])


def _from_time_major(buf_ref, s, b, batch, t_blk):
    return buf_ref[s, pl.ds(b, t_blk, stride=batch), :]


def _mix0_kernel(x_ref, yb_ref, g_ref, wz_ref, cw_ref, lg_ref, lb_ref, w_ref, mg_ref, wu_ref, wd_ref,
                 o_ref, zbuf_ref, conv_ref, lhs_ref, xmid_ref, hm_ref, act_ref):
    i = pl.program_id(0)
    batch, t_blk, _ = x_ref.shape
    rows = batch * t_blk
    n_slab = CONV_CH // LANES
    slot = i % 2

    @pl.when(i == 0)
    def _():
        zbuf_ref[:, 0:rows, :] = jnp.zeros((n_slab, rows, LANES), F32)
        xmid_ref[1] = jnp.zeros((rows, D_MODEL), F32)

    @pl.when(i > 0)
    def _():
        zbuf_ref[:, 0:rows, :] = zbuf_ref[:, rows:2 * rows, :]

    x = x_ref[...].reshape(rows, D_MODEL)
    h = _rmsnorm(x, g_ref[...]).astype(BF16)

    glu_in = {}
    stats = {}
    mine = xmid_ref.at[slot]

    def project(s):
        glu_in[s] = _dot(h, wz_ref[:, 2 * s * LANES:2 * (s + 1) * LANES])

    tokens = {}

    def conv(s):
        u = glu_in.pop(s)
        z = u[:, 0:LANES] * jax.nn.sigmoid(u[:, LANES:2 * LANES])
        _to_time_major(z, zbuf_ref, s, rows, batch, t_blk)
        for r0 in range(0, rows, CONV_ROWS):
            q = s * (rows // CONV_ROWS) + r0 // CONV_ROWS
            tok = tokens[1 + (5 * q) // 4]
            zero = jnp.where(jnp.abs(tok) < jnp.inf, tok - tok, 0.0)
            acc = jnp.concatenate([zero] * (CONV_ROWS // SUBLANES), axis=0)
            for j in range(CONV_WIDTH):
                start = r0 + rows - (CONV_WIDTH - 1 - j) * batch
                acc = acc + zbuf_ref[s, start:start + CONV_ROWS, :] * cw_ref[j:j + 1, s * LANES:(s + 1) * LANES]
            conv_ref[s, r0:r0 + CONV_ROWS, :] = acc

    def attention_part(j):
        cols = slice(j * CONV_CH, (j + 1) * CONV_CH)
        yb = yb_ref[...].reshape(rows, ATTN_WIDTH)
        mine[:, cols] = x_ref[:, :, cols].reshape(rows, CONV_CH) + _dot(yb, w_ref[CONV_CH:CONV_CH + ATTN_WIDTH, cols])

    def layernorm_stats():
        y = [conv_ref[s] for s in range(n_slab)]
        mu = jnp.sum(y[0] + y[1] + y[2] + y[3], axis=-1, keepdims=True) * (1.0 / CONV_CH)
        d = [ys - mu for ys in y]
        var = jnp.sum(d[0] * d[0] + d[1] * d[1] + d[2] * d[2] + d[3] * d[3], axis=-1, keepdims=True) * (1.0 / CONV_CH)
        stats["mu"], stats["inv"] = mu, lax.rsqrt(var + EPS)

    def normalize(s):
        cols = slice(s * LANES, (s + 1) * LANES)
        yn = (conv_ref[s] - stats["mu"]) * stats["inv"] * lg_ref[:, cols] + lb_ref[:, cols]
        conv_ref[s] = yn * jax.nn.sigmoid(yn)
        for b in range(batch):
            lhs_ref[b * t_blk:(b + 1) * t_blk, cols] = _from_time_major(conv_ref, s, b, batch, t_blk).astype(BF16)

    def conv_part(first):
        k_rows = slice(first * LANES, (first + 2) * LANES)
        mine[...] = mine[...] + _dot(lhs_ref[:, k_rows], w_ref[k_rows, :])

    mlp = _mlp_stages(xmid_ref.at[1 - slot], mg_ref, wu_ref, wd_ref, None, o_ref, hm_ref, act_ref)
    dense = [functools.partial(stage, c, j) for c in range(mlp.n_chunks)
             for stage in (mlp.up, mlp.down) for j in range(mlp.n_sub)]
    mixer_after = {0: lambda: project(1), 5: lambda: conv(0), 3: lambda: project(2), 10: lambda: conv(1),
                   7: lambda: project(3), 15: lambda: conv(2), 11: lambda: attention_part(0),
                   20: lambda: conv(3), 13: lambda: attention_part(1), 21: layernorm_stats,
                   22: lambda: normalize(0), 23: lambda: normalize(1), 24: lambda: conv_part(0),
                   25: lambda: normalize(2), 26: lambda: normalize(3), 27: lambda: conv_part(2)}
    assert max(mixer_after) < len(dense)
    mlp.norm()
    project(0)
    for k, dense_stage in enumerate(dense):
        tokens[k] = dense_stage()
        if k in mixer_after:
            mixer_after[k]()


def _layer0_tail(x, yb, g, w_z, conv_w, ln_g, ln_b, w_out, mlp_g, w_up, w_down):
    batch, seq, _ = x.shape
    t_blk = ROW_TILE // batch
    n_blk = seq // t_blk
    assert (CONV_WIDTH - 1) * batch <= ROW_TILE and CONV_CH == 4 * LANES
    blk = lambda width: pl.BlockSpec((batch, t_blk, width), lambda i: (0, jnp.minimum(i, n_blk - 1), 0))
    vec = _const_spec((1, CONV_CH))
    return pl.pallas_call(
        _mix0_kernel,
        grid=(n_blk + 1,),
        in_specs=[blk(D_MODEL), blk(ATTN_WIDTH), _const_spec((1, D_MODEL)), _const_spec(w_z.shape),
                  _const_spec(conv_w.shape), vec, vec, _const_spec(w_out.shape),
                  _const_spec((1, D_MODEL)), _const_spec(w_up.shape), _const_spec(w_down.shape)],
        out_specs=pl.BlockSpec((batch, t_blk, D_MODEL), lambda i: (0, jnp.maximum(i - 1, 0), 0)),
        out_shape=jax.ShapeDtypeStruct(x.shape, F32),
        scratch_shapes=[pltpu.VMEM((CONV_CH // LANES, 2 * ROW_TILE, LANES), F32),
                        pltpu.VMEM((CONV_CH // LANES, ROW_TILE, LANES), F32),
                        pltpu.VMEM((ROW_TILE, CONV_CH), BF16),
                        pltpu.VMEM((2, ROW_TILE, D_MODEL), F32),
                        pltpu.VMEM((ROW_TILE, D_MODEL), BF16),
                        pltpu.VMEM((2, ROW_TILE, MLP_CHUNK), BF16)],
        compiler_params=_params(1),
        name="layer0_tail",
    )(x, yb, g, w_z, conv_w, ln_g, ln_b, w_out, mlp_g, w_up, w_down)


class _mlp_stages:
    n_chunks = D_FF // MLP_CHUNK
    n_sub = MLP_CHUNK // MLP_SUB

    def __init__(self, x_ref, g_ref, wu_ref, wd_ref, final_g_ref, o_ref, h_ref, act_ref):
        self.x_ref, self.g_ref, self.wu_ref, self.wd_ref = x_ref, g_ref, wu_ref, wd_ref
        self.final_g_ref, self.o_ref, self.h_ref, self.act_ref = final_g_ref, o_ref, h_ref, act_ref

    def norm(self):
        self.h_ref[...] = _rmsnorm(self.x_ref[...], self.g_ref[...]).astype(BF16)

    def up(self, c, j):
        cols = slice(j * MLP_SUB, (j + 1) * MLP_SUB)
        a = _dot(self.h_ref[...], self.wu_ref[:, c * MLP_CHUNK + j * MLP_SUB:c * MLP_CHUNK + (j + 1) * MLP_SUB])
        self.act_ref[c % 2, :, cols] = jnp.square(jnp.maximum(a, 0.0)).astype(BF16)
        return self.act_ref[c % 2, 0:BF16_ROWS, j * MLP_SUB:j * MLP_SUB + LANES].astype(F32)[0:SUBLANES]

    def down(self, c, j):
        cols = slice(j * MLP_SUB, (j + 1) * MLP_SUB)
        batch, t_blk, _ = self.o_ref.shape
        part = _dot(self.act_ref[c % 2], self.wd_ref[c * MLP_CHUNK:(c + 1) * MLP_CHUNK, cols])
        acc = self.x_ref[:, cols] if c == 0 else self.o_ref[:, :, cols].reshape(part.shape)
        self.o_ref[:, :, cols] = (acc + part).reshape(batch, t_blk, MLP_SUB)
        return self.o_ref[0, 0:SUBLANES, j * MLP_SUB:j * MLP_SUB + LANES]

    def finish(self):
        if self.final_g_ref is not None:
            shape = self.o_ref.shape
            acc = self.o_ref[...].reshape(shape[0] * shape[1], shape[2])
            self.o_ref[...] = _rmsnorm(acc, self.final_g_ref[...]).reshape(shape)


def _gelu_tanh(x):
    return 0.5 * x * (1.0 + jnp.tanh(0.7978845608028654 * (x + 0.044715 * (x * x * x))))


def _lru_kernel(x_ref, g_ref, win_ref, cw_ref, cb_ref, wri_ref, br_ref, bi_ref, lam_ref, wout_ref,
                mg_ref, wu_ref, wd_ref, fg_ref,
                o_ref, xbuf_ref, a_ref, b_ref, gate_ref, h_ref, y_ref, xmid_ref, hm_ref, act_ref):
    i = pl.program_id(0)
    batch, t_blk, _ = x_ref.shape
    rows = batch * t_blk
    halo = xbuf_ref.shape[1] - rows
    slot = i % 2

    @pl.when(i == 0)
    def _():
        xbuf_ref[:, 0:halo, :] = jnp.zeros((LRU_HEADS, halo, LANES), F32)
        h_ref[...] = jnp.zeros(h_ref.shape, F32)
        xmid_ref[1] = jnp.zeros((rows, D_MODEL), F32)

    @pl.when(i > 0)
    def _():
        xbuf_ref[:, 0:halo, :] = xbuf_ref[:, rows:rows + halo, :]

    x = x_ref[...].reshape(rows, D_MODEL)
    h = _rmsnorm(x, g_ref[...]).astype(BF16)
    neg_lam = -lam_ref[...]
    softplus = jnp.maximum(neg_lam, 0.0) + jnp.log1p(jnp.exp(-jnp.abs(neg_lam)))
    group = 2
    width = group * LRU_BLOCK

    def project_gate(p):
        cols = slice(p * width, (p + 1) * width)
        gate_ref[:, cols] = _gelu_tanh(_dot(h, win_ref[:, cols]))

    def project_x(p):
        xr = _dot(h, win_ref[:, D_MODEL + p * width:D_MODEL + (p + 1) * width])
        _to_time_major(xr, xbuf_ref, p * group, halo, batch, t_blk)

    def recurrence_inputs(hd):
        cols = slice(hd * LRU_BLOCK, (hd + 1) * LRU_BLOCK)
        xc = cb_ref[:, cols]
        for j in range(LRU_CONV_WIDTH):
            start = halo - (LRU_CONV_WIDTH - 1 - j) * batch
            xc = xc + xbuf_ref[hd, start:start + rows, :] * cw_ref[j:j + 1, cols]
        ri = _dot(xc.astype(BF16), wri_ref[hd])
        r = jax.nn.sigmoid(ri[:, 0:LRU_BLOCK] + br_ref[:, cols])
        ig = jax.nn.sigmoid(ri[:, LRU_BLOCK:2 * LRU_BLOCK] + bi_ref[:, cols])
        log_a = -LRU_C * r * softplus[:, cols]
        a = jnp.exp(log_a)
        a_ref[hd] = a
        b_ref[hd] = jnp.sqrt(-jnp.tanh(log_a) * (a * a + 1.0)) * (ig * xc)

    def scan(p):
        heads = range(p * group, (p + 1) * group)
        state = {hd: h_ref[hd] for hd in heads}
        for t in range(t_blk):
            step = slice(t * batch, (t + 1) * batch)
            for hd in heads:
                state[hd] = a_ref[hd, step, :] * state[hd] + b_ref[hd, step, :]
                b_ref[hd, step, :] = state[hd]
        for hd in heads:
            h_ref[hd] = state[hd]
            cols = slice(hd * LRU_BLOCK, (hd + 1) * LRU_BLOCK)
            for b in range(batch):
                blk = slice(b * t_blk, (b + 1) * t_blk)
                y_ref[blk, cols] = (_from_time_major(b_ref, hd, b, batch, t_blk) * gate_ref[blk, cols]).astype(BF16)

    def out_part(p):
        k_rows = slice(p * width, (p + 1) * width)
        return _dot(y_ref[:, k_rows], wout_ref[k_rows, :])

    n_groups = LRU_HEADS // group
    mlp = _mlp_stages(xmid_ref.at[1 - slot], mg_ref, wu_ref, wd_ref, fg_ref, o_ref, hm_ref, act_ref)
    assert n_groups == mlp.n_chunks and mlp.n_sub == 4 and group == 2
    out = [x]

    def add_out_part(p):
        out[0] = out[0] + out_part(p)

    mlp.norm()
    project_x(0)
    project_gate(0)
    for p in range(n_groups):
        last = p + 1 == n_groups
        mixer = [lambda: recurrence_inputs(p * group),
                 lambda: recurrence_inputs(p * group + 1),
                 (lambda: None) if last else (lambda: project_x(p + 1)),
                 (lambda: None) if last else (lambda: project_gate(p + 1)),
                 lambda: scan(p),
                 (lambda: add_out_part(p - 1)) if p > 0 else (lambda: None)]
        dense = [lambda j=j: mlp.up(p, j) for j in range(mlp.n_sub)]
        dense += [lambda j=j: mlp.down(p, j) for j in range(mlp.n_sub)]
        for k, dense_stage in enumerate(dense):
            dense_stage()
            if k < len(mixer):
                mixer[k]()
    xmid_ref[slot] = out[0] + out_part(n_groups - 1)
    mlp.finish()


def _layer1(x, g, w_in, conv_w, conv_b, w_ri, b_r, b_i, lam, w_out, mlp_g, w_up, w_down, final_g):
    batch, seq, _ = x.shape
    t_blk = ROW_TILE // batch
    n_blk = seq // t_blk
    halo = 64
    assert (LRU_CONV_WIDTH - 1) * batch <= halo and LRU_BLOCK == LANES
    vec = _const_spec((1, D_MODEL))
    slab = lambda n: pltpu.VMEM((LRU_HEADS, n, LANES), F32)
    return pl.pallas_call(
        _lru_kernel,
        grid=(n_blk + 1,),
        in_specs=[pl.BlockSpec((batch, t_blk, D_MODEL), lambda i: (0, jnp.minimum(i, n_blk - 1), 0)),
                  vec, _const_spec(w_in.shape), _const_spec(conv_w.shape), vec,
                  _const_spec(w_ri.shape), vec, vec, vec, _const_spec(w_out.shape),
                  vec, _const_spec(w_up.shape), _const_spec(w_down.shape), vec],
        out_specs=pl.BlockSpec((batch, t_blk, D_MODEL), lambda i: (0, jnp.maximum(i - 1, 0), 0)),
        out_shape=jax.ShapeDtypeStruct(x.shape, F32),
        scratch_shapes=[slab(halo + ROW_TILE), slab(ROW_TILE), slab(ROW_TILE),
                        pltpu.VMEM((ROW_TILE, D_MODEL), F32), slab(batch),
                        pltpu.VMEM((ROW_TILE, D_MODEL), BF16),
                        pltpu.VMEM((2, ROW_TILE, D_MODEL), F32),
                        pltpu.VMEM((ROW_TILE, D_MODEL), BF16),
                        pltpu.VMEM((2, ROW_TILE, MLP_CHUNK), BF16)],
        compiler_params=_params(1),
        name="layer1",
    )(x, g, w_in, conv_w, conv_b, w_ri, b_r, b_i, lam, w_out, mlp_g, w_up, w_down, final_g)


def kernel(x, mix_norm, mlp_norm, w_up, w_down, ab_w_in, ab_conv_w, ab_ln_g, ab_ln_b, ab_w_out,
           c_w_in, c_conv_w, c_conv_b, c_w_r, c_b_r, c_w_i, c_b_i, c_lambda, c_w_out, final_norm):
    batch, seq, d = x.shape
    vec = lambda a: a.reshape(1, -1).astype(F32)

    w_in = ab_w_in[0]
    o = 2 * CONV_CH
    w_q, w_k, w_v = (w_in[:, o + n * ATTN_WIDTH:o + (n + 1) * ATTN_WIDTH] for n in range(3))
    k, qt, vt = _qkv(x, vec(mix_norm[0]), w_k.astype(BF16), w_q.T.astype(BF16), w_v.T.astype(BF16))
    yb = _attention(qt, k, vt)
    conv_w = jnp.pad(ab_conv_w[0], ((0, 32 - CONV_WIDTH), (0, 0)))
    w_glu = w_in[:, 0:o].reshape(d, 2, CONV_CH // LANES, LANES).transpose(0, 2, 1, 3).reshape(d, o)
    x = _layer0_tail(x, yb, vec(mix_norm[0]), w_glu.astype(BF16), conv_w, vec(ab_ln_g[0]), vec(ab_ln_b[0]),
                     ab_w_out[0].astype(BF16), vec(mlp_norm[0]), w_up[0].astype(BF16), w_down[0].astype(BF16))

    w_ri = jnp.concatenate([c_w_r[0], c_w_i[0]], axis=-1).astype(BF16)
    lru_conv_w = jnp.pad(c_conv_w[0], ((0, SUBLANES - LRU_CONV_WIDTH), (0, 0)))
    return _layer1(x, vec(mix_norm[1]), c_w_in[0].astype(BF16), lru_conv_w, vec(c_conv_b[0]), w_ri,
                   vec(c_b_r[0]), vec(c_b_i[0]), vec(c_lambda[0]), c_w_out[0].astype(BF16),
                   vec(mlp_norm[1]), w_up[1].astype(BF16), w_down[1].astype(BF16), vec(final_norm))
```

```python
import functools

import jax
import jax.numpy as jnp
from jax import lax
from jax.experimental import pallas as pl
from jax.experimental.pallas import tpu as pltpu

F32 = jnp.float32
BF16 = jnp.bfloat16

EPS = 1e-6
D_MODEL = 1024
D_FF = 4 * D_MODEL
CONV_CH = 512
CONV_WIDTH = 31
ATTN_WIDTH = 512
HEAD_DIM = 64
N_HEADS = 8
MOBA_BLOCK = 256
MOBA_TOP_K = 3
LRU_HEADS = 8
LRU_BLOCK = 128
LRU_CONV_WIDTH = 4
LRU_C = 8.0

LANES = 128
SUBLANES = 8
BF16_ROWS = 16
HEADS_PER_STEP = LANES // HEAD_DIM
MASKED = -3e38
VMEM_LIMIT = 56 * 1024 * 1024

ROW_TILE = 512
QKV_ROWS = 2048
MLP_CHUNK = 1024
MLP_SUB = 256
CONV_ROWS = 128


def _dot(a, b):
    return jnp.dot(a, b, preferred_element_type=F32)


def _dot_nt(a, b):
    return lax.dot_general(a, b, (((1,), (1,)), ((), ())), preferred_element_type=F32)


def _rmsnorm(x, g):
    return x * lax.rsqrt(jnp.mean(x * x, axis=-1, keepdims=True) + EPS) * g


def _const_spec(shape):
    zeros = (0,) * len(shape)
    return pl.BlockSpec(shape, lambda *_: zeros, pipeline_mode=pl.Buffered(1))


def _params(n_axes):
    return pltpu.CompilerParams(dimension_semantics=("arbitrary",) * n_axes,
                                vmem_limit_bytes=VMEM_LIMIT)


def _qkv_kernel(x_ref, g_ref, wk_ref, wqt_ref, wvt_ref, k_ref, qt_ref, vt_ref):
    h = _rmsnorm(x_ref[...], g_ref[...]).astype(BF16)
    k_ref[...] = _dot(h, wk_ref[...]).astype(BF16)
    qt_ref[...] = _dot_nt(wqt_ref[...], h).astype(BF16)
    vt_ref[...] = _dot_nt(wvt_ref[...], h).astype(BF16)


def _qkv(x, g, wk, wqt, wvt):
    batch, seq, _ = x.shape
    w_spec = _const_spec((D_MODEL, ATTN_WIDTH))
    wt_spec = _const_spec((ATTN_WIDTH, D_MODEL))
    t_spec = pl.BlockSpec((None, ATTN_WIDTH, QKV_ROWS), lambda b, t: (b, 0, t))
    t_shape = jax.ShapeDtypeStruct((batch, ATTN_WIDTH, seq), BF16)
    return pl.pallas_call(
        _qkv_kernel,
        grid=(batch, seq // QKV_ROWS),
        in_specs=[pl.BlockSpec((None, QKV_ROWS, D_MODEL), lambda b, t: (b, t, 0)),
                  _const_spec((1, D_MODEL)), w_spec, wt_spec, wt_spec],
        out_specs=[pl.BlockSpec((None, QKV_ROWS, ATTN_WIDTH), lambda b, t: (b, t, 0)), t_spec, t_spec],
        out_shape=[jax.ShapeDtypeStruct((batch, seq, ATTN_WIDTH), BF16), t_shape, t_shape],
        compiler_params=_params(2),
        name="qkv_proj",
    )(x, g, wk, wqt, wvt)


def _head_lanes(hh):
    qk0 = hh * HEAD_DIM
    ex0 = (1 - hh) * HEAD_DIM
    return qk0, ex0, ex0 + BF16_ROWS


V_ROWS = HEAD_DIM + BF16_ROWS


def _attn_prepare(k_ref, vt_ref, slope_ref, kaug_ref, kbar_ref, vta_ref):
    n_keys = k_ref.shape[0]
    rowk = lax.broadcasted_iota(jnp.int32, (n_keys, LANES), 0)
    lane = lax.broadcasted_iota(jnp.int32, (n_keys, LANES), 1)
    tk = (rowk & (MOBA_BLOCK - 1)).astype(F32)
    blk = rowk >> (MOBA_BLOCK.bit_length() - 1)
    ones_rows = jnp.where(lax.broadcasted_iota(jnp.int32, (BF16_ROWS, n_keys), 0) == 0, 1.0, 0.0)
    for h in range(N_HEADS):
        pair, hh = divmod(h, HEADS_PER_STEP)
        qk0, ex0, bias0 = _head_lanes(hh)
        slope = slope_ref[h:h + 1, 0:LANES]
        extra = jnp.where(lane == ex0, 1.0,
                          jnp.where(lane == ex0 + 1, slope * tk,
                                    jnp.where(lane - bias0 == blk, 1.0, 0.0)))
        is_k = (lane >= qk0) & (lane < qk0 + HEAD_DIM)
        kaug_ref[h] = jnp.where(is_k, k_ref[:, pair * LANES:(pair + 1) * LANES], extra.astype(BF16))
        vta_ref[h, 0:HEAD_DIM, :] = vt_ref[h * HEAD_DIM:(h + 1) * HEAD_DIM, :]
        vta_ref[h, HEAD_DIM:V_ROWS, :] = ones_rows.astype(BF16)
    for j in range(n_keys // MOBA_BLOCK):
        kj = k_ref[j * MOBA_BLOCK:(j + 1) * MOBA_BLOCK, :].astype(F32)
        kbar_ref[j:j + 1, :] = jnp.sum(kj, axis=0, keepdims=True) * (1.0 / MOBA_BLOCK)


def _attn_block(c, qt_ref, slope_ref, kaug_ref, kbar_ref, vta_ref, o_ref):
    rowi = lax.broadcasted_iota(jnp.int32, (SUBLANES, MOBA_BLOCK), 0)
    tq = lax.broadcasted_iota(jnp.int32, (SUBLANES, MOBA_BLOCK), 1).astype(F32)
    causal = (lax.broadcasted_iota(jnp.int32, (MOBA_BLOCK, MOBA_BLOCK), 0)
              <= lax.broadcasted_iota(jnp.int32, (MOBA_BLOCK, MOBA_BLOCK), 1))
    zeros8 = jnp.zeros((SUBLANES, MOBA_BLOCK), F32)
    lane_k = lax.broadcasted_iota(jnp.int32, (SUBLANES, LANES), 1)
    n_past = c * MOBA_BLOCK
    own = slice(n_past, n_past + MOBA_BLOCK)

    qats = []
    for pair in range(N_HEADS // HEADS_PER_STEP):
        qt = qt_ref[pair * LANES:(pair + 1) * LANES, :]
        if c > 0:
            kbar = kbar_ref[:, pair * LANES:(pair + 1) * LANES]
            kbar = jnp.concatenate([jnp.where(lane_k < HEAD_DIM, kbar, 0.0),
                                    jnp.where(lane_k >= HEAD_DIM, kbar, 0.0)], axis=0)
            kb_hi = kbar.astype(BF16)
            kb_lo = (kbar - kb_hi.astype(F32)).astype(BF16)
            gates = _dot(kb_hi, qt) + _dot(kb_lo, qt)
        for hh in range(HEADS_PER_STEP):
            h = pair * HEADS_PER_STEP + hh
            qk0, ex0, bias0 = _head_lanes(hh)
            slope = slope_ref[h:h + 1, :]
            if c > 0:
                gate = gates[hh * SUBLANES:(hh + 1) * SUBLANES, :]
                elig = rowi < c
                sel = elig & (jnp.abs(gate) < jnp.inf)
                if c > MOBA_TOP_K:
                    beaten = zeros8
                    for jp in range(c):
                        gp = gate[jp:jp + 1, :]
                        beats = (gp > gate) | ((gp == gate) & (rowi > jp))
                        beaten = beaten + jnp.where(beats, 1.0, 0.0)
                    sel = sel & (beaten < float(MOBA_TOP_K))
                dist = (c - rowi).astype(F32) * float(MOBA_BLOCK)
                bias = jnp.where(elig, jnp.where(sel, -slope * dist, MASKED), 0.0)
            else:
                bias = zeros8
            ext_a = jnp.where(rowi == 0, -slope * tq, jnp.where(rowi == 1, 1.0, 0.0))
            ext = jnp.concatenate([ext_a, zeros8, bias, zeros8], axis=0).astype(BF16)
            pad = jnp.zeros((HEAD_DIM - ext.shape[0], MOBA_BLOCK), BF16)
            qs = qt[qk0:qk0 + HEAD_DIM, :] * jnp.asarray(HEAD_DIM ** -0.5, BF16)
            qats.append(jnp.concatenate([qs, ext, pad] if hh == 0 else [ext, pad, qs], axis=0))

    def logits(h):
        s_own = jnp.where(causal, _dot(kaug_ref[h, own, :], qats[h]), -jnp.inf)
        m = jnp.max(s_own, axis=0, keepdims=True)
        s_past = None
        if c > 0:
            s_past = _dot(kaug_ref[h, 0:n_past, :], qats[h])
            m = jnp.maximum(m, jnp.max(s_past, axis=0, keepdims=True))
        return s_own, s_past, m

    def weighted_values(h, s_own, s_past, m):
        acc = _dot(vta_ref[h, :, own], jnp.exp((s_own - m).astype(BF16)))
        if c > 0:
            acc = acc + _dot(vta_ref[h, :, 0:n_past], jnp.exp((s_past - m).astype(BF16)))
        return acc[0:HEAD_DIM, :] * (1.0 / acc[HEAD_DIM:HEAD_DIM + 1, :])

    def store(outs):
        for pair in range(N_HEADS // HEADS_PER_STEP):
            o_pair = jnp.concatenate(outs[pair * HEADS_PER_STEP:(pair + 1) * HEADS_PER_STEP], axis=0)
            o_ref[:, pair * LANES:(pair + 1) * LANES] = o_pair.T.astype(o_ref.dtype)

    return logits, weighted_values, store


ATTN_AHEAD = 3


def _attn_kernel(slope_ref, qt_ref, k_ref, vt_ref, o_ref, kaug_ref, kbar_ref, vta_ref):
    _attn_prepare(k_ref, vt_ref, slope_ref, kaug_ref, kbar_ref, vta_ref)
    n_blk = k_ref.shape[0] // MOBA_BLOCK
    items = [(c, h) for c in range(n_blk) for h in range(N_HEADS)]
    blocks, pending, outs = {}, [], {c: [] for c in range(n_blk)}

    def issue(c, h):
        if c not in blocks:
            blk = slice(c * MOBA_BLOCK, (c + 1) * MOBA_BLOCK)
            blocks[c] = _attn_block(c, qt_ref.at[:, blk], slope_ref, kaug_ref, kbar_ref, vta_ref, o_ref.at[blk])
        pending.append(blocks[c][0](h))

    for c, h in items[:ATTN_AHEAD]:
        issue(c, h)
    for n, (c, h) in enumerate(items):
        if n + ATTN_AHEAD < len(items):
            issue(*items[n + ATTN_AHEAD])
        outs[c].append(blocks[c][1](h, *pending.pop(0)))
        if h == N_HEADS - 1:
            blocks[c][2](outs.pop(c))


def _attention(qt, k, vt):
    batch, seq, _ = k.shape
    n_blk = seq // MOBA_BLOCK
    slopes = 2.0 ** (-8.0 * jnp.arange(1, N_HEADS + 1, dtype=F32) / N_HEADS)
    slopes = jnp.broadcast_to(slopes.reshape(N_HEADS, 1), (N_HEADS, MOBA_BLOCK))
    return pl.pallas_call(
        _attn_kernel,
        grid=(batch,),
        in_specs=[
            _const_spec((N_HEADS, MOBA_BLOCK)),
            pl.BlockSpec((None, ATTN_WIDTH, seq), lambda b: (b, 0, 0)),
            pl.BlockSpec((None, seq, ATTN_WIDTH), lambda b: (b, 0, 0)),
            pl.BlockSpec((None, ATTN_WIDTH, seq), lambda b: (b, 0, 0)),
        ],
        out_specs=pl.BlockSpec((None, seq, ATTN_WIDTH), lambda b: (b, 0, 0)),
        out_shape=jax.ShapeDtypeStruct((batch, seq, ATTN_WIDTH), BF16),
        scratch_shapes=[pltpu.VMEM((N_HEADS, seq, LANES), BF16),
                        pltpu.VMEM((n_blk, ATTN_WIDTH), F32),
                        pltpu.VMEM((N_HEADS, V_ROWS, seq), BF16)],
        compiler_params=_params(1),
        name="moba_attention",
    )(slopes, qt, k, vt)


def _to_time_major(val, buf_ref, slab0, row0, batch, t_blk):
    for s in range(val.shape[1] // LANES):
        for b in range(batch):
            buf_ref[slab0 + s, pl.ds(row0 + b, t_blk, stride=batch), :] = (
                val[b * t_blk:(b + 1) * t_blk, s * LANES:(s + 1) * LANES])


def _from_time_major(buf_ref, s, b, batch, t_blk):
    return buf_ref[s, pl.ds(b, t_blk, stride=batch), :]


def _mix0_kernel(x_ref, yb_ref, g_ref, wz_ref, cw_ref, lg_ref, lb_ref, w_ref, mg_ref, wu_ref, wd_ref,
                 o_ref, zbuf_ref, conv_ref, lhs_ref, xmid_ref, hm_ref, act_ref):
    i = pl.program_id(0)
    batch, t_blk, _ = x_ref.shape
    rows = batch * t_blk
    n_slab = CONV_CH // LANES
    slot = i % 2

    @pl.when(i == 0)
    def _():
        zbuf_ref[:, 0:rows, :] = jnp.zeros((n_slab, rows, LANES), F32)
        xmid_ref[1] = jnp.zeros((rows, D_MODEL), F32)

    @pl.when(i > 0)
    def _():
        zbuf_ref[:, 0:rows, :] = zbuf_ref[:, rows:2 * rows, :]

    x = x_ref[...].reshape(rows, D_MODEL)
    h = _rmsnorm(x, g_ref[...]).astype(BF16)

    glu_in = {}
    stats = {}
    mine = xmid_ref.at[slot]

    def project(s):
        glu_in[s] = _dot(h, wz_ref[:, 2 * s * LANES:2 * (s + 1) * LANES])

    tokens = {}

    def conv(s):
        u = glu_in.pop(s)
        z = u[:, 0:LANES] * jax.nn.sigmoid(u[:, LANES:2 * LANES])
        _to_time_major(z, zbuf_ref, s, rows, batch, t_blk)
        for r0 in range(0, rows, CONV_ROWS):
            q = s * (rows // CONV_ROWS) + r0 // CONV_ROWS
            tok = tokens[1 + (4 * q) // 3]
            zero = jnp.where(jnp.abs(tok) < jnp.inf, tok - tok, 0.0)
            acc = jnp.concatenate([zero] * (CONV_ROWS // SUBLANES), axis=0)
            for j in range(CONV_WIDTH):
                start = r0 + rows - (CONV_WIDTH - 1 - j) * batch
                acc = acc + zbuf_ref[s, start:start + CONV_ROWS, :] * cw_ref[j:j + 1, s * LANES:(s + 1) * LANES]
            conv_ref[s, r0:r0 + CONV_ROWS, :] = acc

    def attention_part(j):
        cols = slice(j * CONV_CH, (j + 1) * CONV_CH)
        yb = yb_ref[...].reshape(rows, ATTN_WIDTH)
        mine[:, cols] = x_ref[:, :, cols].reshape(rows, CONV_CH) + _dot(yb, w_ref[CONV_CH:CONV_CH + ATTN_WIDTH, cols])

    def layernorm_stats():
        y = [conv_ref[s] for s in range(n_slab)]
        mu = jnp.sum(y[0] + y[1] + y[2] + y[3], axis=-1, keepdims=True) * (1.0 / CONV_CH)
        d = [ys - mu for ys in y]
        var = jnp.sum(d[0] * d[0] + d[1] * d[1] + d[2] * d[2] + d[3] * d[3], axis=-1, keepdims=True) * (1.0 / CONV_CH)
        stats["mu"], stats["inv"] = mu, lax.rsqrt(var + EPS)

    def normalize(s):
        cols = slice(s * LANES, (s + 1) * LANES)
        yn = (conv_ref[s] - stats["mu"]) * stats["inv"] * lg_ref[:, cols] + lb_ref[:, cols]
        conv_ref[s] = yn * jax.nn.sigmoid(yn)
        for b in range(batch):
            lhs_ref[b * t_blk:(b + 1) * t_blk, cols] = _from_time_major(conv_ref, s, b, batch, t_blk).astype(BF16)

    def conv_part(first):
        k_rows = slice(first * LANES, (first + 2) * LANES)
        mine[...] = mine[...] + _dot(lhs_ref[:, k_rows], w_ref[k_rows, :])

    mlp = _mlp_stages(xmid_ref.at[1 - slot], mg_ref, wu_ref, wd_ref, None, o_ref, hm_ref, act_ref)
    mlp.norm()
    for c in range(mlp.n_chunks):
        for stage in (mlp.up, mlp.down):
            for j in range(mlp.n_sub):
                tokens[len(tokens)] = stage(c, j)
    for s in range(n_slab):
        project(s)
        conv(s)
    attention_part(0)
    attention_part(1)
    layernorm_stats()
    for first in range(0, n_slab, 2):
        normalize(first)
        normalize(first + 1)
        conv_part(first)


def _layer0_tail(x, yb, g, w_z, conv_w, ln_g, ln_b, w_out, mlp_g, w_up, w_down):
    batch, seq, _ = x.shape
    t_blk = ROW_TILE // batch
    n_blk = seq // t_blk
    assert (CONV_WIDTH - 1) * batch <= ROW_TILE and CONV_CH == 4 * LANES
    blk = lambda width: pl.BlockSpec((batch, t_blk, width), lambda i: (0, jnp.minimum(i, n_blk - 1), 0))
    vec = _const_spec((1, CONV_CH))
    return pl.pallas_call(
        _mix0_kernel,
        grid=(n_blk + 1,),
        in_specs=[blk(D_MODEL), blk(ATTN_WIDTH), _const_spec((1, D_MODEL)), _const_spec(w_z.shape),
                  _const_spec(conv_w.shape), vec, vec, _const_spec(w_out.shape),
                  _const_spec((1, D_MODEL)), _const_spec(w_up.shape), _const_spec(w_down.shape)],
        out_specs=pl.BlockSpec((batch, t_blk, D_MODEL), lambda i: (0, jnp.maximum(i - 1, 0), 0)),
        out_shape=jax.ShapeDtypeStruct(x.shape, F32),
        scratch_shapes=[pltpu.VMEM((CONV_CH // LANES, 2 * ROW_TILE, LANES), F32),
                        pltpu.VMEM((CONV_CH // LANES, ROW_TILE, LANES), F32),
                        pltpu.VMEM((ROW_TILE, CONV_CH), BF16),
                        pltpu.VMEM((2, ROW_TILE, D_MODEL), F32),
                        pltpu.VMEM((ROW_TILE, D_MODEL), BF16),
                        pltpu.VMEM((2, ROW_TILE, MLP_CHUNK), BF16)],
        compiler_params=_params(1),
        name="layer0_tail",
    )(x, yb, g, w_z, conv_w, ln_g, ln_b, w_out, mlp_g, w_up, w_down)


class _mlp_stages:
    n_chunks = D_FF // MLP_CHUNK
    n_sub = MLP_CHUNK // MLP_SUB

    def __init__(self, x_ref, g_ref, wu_ref, wd_ref, final_g_ref, o_ref, h_ref, act_ref):
        self.x_ref, self.g_ref, self.wu_ref, self.wd_ref = x_ref, g_ref, wu_ref, wd_ref
        self.final_g_ref, self.o_ref, self.h_ref, self.act_ref = final_g_ref, o_ref, h_ref, act_ref

    def norm(self):
        self.h_ref[...] = _rmsnorm(self.x_ref[...], self.g_ref[...]).astype(BF16)

    def up(self, c, j):
        cols = slice(j * MLP_SUB, (j + 1) * MLP_SUB)
        a = _dot(self.h_ref[...], self.wu_ref[:, c * MLP_CHUNK + j * MLP_SUB:c * MLP_CHUNK + (j + 1) * MLP_SUB])
        self.act_ref[c % 2, :, cols] = jnp.square(jnp.maximum(a, 0.0)).astype(BF16)
        return self.act_ref[c % 2, 0:BF16_ROWS, j * MLP_SUB:j * MLP_SUB + LANES].astype(F32)[0:SUBLANES]

    def down(self, c, j):
        cols = slice(j * MLP_SUB, (j + 1) * MLP_SUB)
        batch, t_blk, _ = self.o_ref.shape
        part = _dot(self.act_ref[c % 2], self.wd_ref[c * MLP_CHUNK:(c + 1) * MLP_CHUNK, cols])
        acc = self.x_ref[:, cols] if c == 0 else self.o_ref[:, :, cols].reshape(part.shape)
        self.o_ref[:, :, cols] = (acc + part).reshape(batch, t_blk, MLP_SUB)
        return self.o_ref[0, 0:SUBLANES, j * MLP_SUB:j * MLP_SUB + LANES]

    def finish(self):
        if self.final_g_ref is not None:
            shape = self.o_ref.shape
            acc = self.o_ref[...].reshape(shape[0] * shape[1], shape[2])
            self.o_ref[...] = _rmsnorm(acc, self.final_g_ref[...]).reshape(shape)


def _gelu_tanh(x):
    return 0.5 * x * (1.0 + jnp.tanh(0.7978845608028654 * (x + 0.044715 * (x * x * x))))


def _lru_kernel(x_ref, g_ref, win_ref, cw_ref, cb_ref, wri_ref, br_ref, bi_ref, lam_ref, wout_ref,
                mg_ref, wu_ref, wd_ref, fg_ref,
                o_ref, xbuf_ref, a_ref, b_ref, gate_ref, h_ref, y_ref, xmid_ref, hm_ref, act_ref):
    i = pl.program_id(0)
    batch, t_blk, _ = x_ref.shape
    rows = batch * t_blk
    halo = xbuf_ref.shape[1] - rows
    slot = i % 2

    @pl.when(i == 0)
    def _():
        xbuf_ref[:, 0:halo, :] = jnp.zeros((LRU_HEADS, halo, LANES), F32)
        h_ref[...] = jnp.zeros(h_ref.shape, F32)
        xmid_ref[1] = jnp.zeros((rows, D_MODEL), F32)

    @pl.when(i > 0)
    def _():
        xbuf_ref[:, 0:halo, :] = xbuf_ref[:, rows:rows + halo, :]

    x = x_ref[...].reshape(rows, D_MODEL)
    h = _rmsnorm(x, g_ref[...]).astype(BF16)
    neg_lam = -lam_ref[...]
    softplus = jnp.maximum(neg_lam, 0.0) + jnp.log1p(jnp.exp(-jnp.abs(neg_lam)))
    group = 2
    width = group * LRU_BLOCK

    def project_gate(p):
        cols = slice(p * width, (p + 1) * width)
        gate_ref[:, cols] = _gelu_tanh(_dot(h, win_ref[:, cols]))

    def project_x(p):
        xr = _dot(h, win_ref[:, D_MODEL + p * width:D_MODEL + (p + 1) * width])
        _to_time_major(xr, xbuf_ref, p * group, halo, batch, t_blk)

    def recurrence_inputs(hd):
        cols = slice(hd * LRU_BLOCK, (hd + 1) * LRU_BLOCK)
        xc = cb_ref[:, cols]
        for j in range(LRU_CONV_WIDTH):
            start = halo - (LRU_CONV_WIDTH - 1 - j) * batch
            xc = xc + xbuf_ref[hd, start:start + rows, :] * cw_ref[j:j + 1, cols]
        ri = _dot(xc.astype(BF16), wri_ref[hd])
        r = jax.nn.sigmoid(ri[:, 0:LRU_BLOCK] + br_ref[:, cols])
        ig = jax.nn.sigmoid(ri[:, LRU_BLOCK:2 * LRU_BLOCK] + bi_ref[:, cols])
        log_a = -LRU_C * r * softplus[:, cols]
        a = jnp.exp(log_a)
        a_ref[hd] = a
        b_ref[hd] = jnp.sqrt(-jnp.tanh(log_a) * (a * a + 1.0)) * (ig * xc)

    def scan(p):
        heads = range(p * group, (p + 1) * group)
        state = {hd: h_ref[hd] for hd in heads}
        for t in range(t_blk):
            step = slice(t * batch, (t + 1) * batch)
            for hd in heads:
                state[hd] = a_ref[hd, step, :] * state[hd] + b_ref[hd, step, :]
                b_ref[hd, step, :] = state[hd]
        for hd in heads:
            h_ref[hd] = state[hd]
            cols = slice(hd * LRU_BLOCK, (hd + 1) * LRU_BLOCK)
            for b in range(batch):
                blk = slice(b * t_blk, (b + 1) * t_blk)
                y_ref[blk, cols] = (_from_time_major(b_ref, hd, b, batch, t_blk) * gate_ref[blk, cols]).astype(BF16)

    def out_part(p):
        k_rows = slice(p * width, (p + 1) * width)
        return _dot(y_ref[:, k_rows], wout_ref[k_rows, :])

    n_groups = LRU_HEADS // group
    mlp = _mlp_stages(xmid_ref.at[1 - slot], mg_ref, wu_ref, wd_ref, fg_ref, o_ref, hm_ref, act_ref)
    assert n_groups == mlp.n_chunks and mlp.n_sub == 4 and group == 2
    out = [x]

    def add_out_part(p):
        out[0] = out[0] + out_part(p)

    mlp.norm()
    project_x(0)
    project_gate(0)
    for p in range(n_groups):
        last = p + 1 == n_groups
        mixer = [lambda: recurrence_inputs(p * group),
                 lambda: recurrence_inputs(p * group + 1),
                 (lambda: None) if last else (lambda: project_x(p + 1)),
                 (lambda: None) if last else (lambda: project_gate(p + 1)),
                 lambda: scan(p),
                 (lambda: add_out_part(p - 1)) if p > 0 else (lambda: None)]
        dense = [lambda j=j: mlp.up(p, j) for j in range(mlp.n_sub)]
        dense += [lambda j=j: mlp.down(p, j) for j in range(mlp.n_sub)]
        for k, dense_stage in enumerate(dense):
            dense_stage()
            if k < len(mixer):
                mixer[k]()
    xmid_ref[slot] = out[0] + out_part(n_groups - 1)
    mlp.finish()


def _layer1(x, g, w_in, conv_w, conv_b, w_ri, b_r, b_i, lam, w_out, mlp_g, w_up, w_down, final_g):
    batch, seq, _ = x.shape
    t_blk = ROW_TILE // batch
    n_blk = seq // t_blk
    halo = 64
    assert (LRU_CONV_WIDTH - 1) * batch <= halo and LRU_BLOCK == LANES
    vec = _const_spec((1, D_MODEL))
    slab = lambda n: pltpu.VMEM((LRU_HEADS, n, LANES), F32)
    return pl.pallas_call(
        _lru_kernel,
        grid=(n_blk + 1,),
        in_specs=[pl.BlockSpec((batch, t_blk, D_MODEL), lambda i: (0, jnp.minimum(i, n_blk - 1), 0)),
                  vec, _const_spec(w_in.shape), _const_spec(conv_w.shape), vec,
                  _const_spec(w_ri.shape), vec, vec, vec, _const_spec(w_out.shape),
                  vec, _const_spec(w_up.shape), _const_spec(w_down.shape), vec],
        out_specs=pl.BlockSpec((batch, t_blk, D_MODEL), lambda i: (0, jnp.maximum(i - 1, 0), 0)),
        out_shape=jax.ShapeDtypeStruct(x.shape, F32),
        scratch_shapes=[slab(halo + ROW_TILE), slab(ROW_TILE), slab(ROW_TILE),
                        pltpu.VMEM((ROW_TILE, D_MODEL), F32), slab(batch),
                        pltpu.VMEM((ROW_TILE, D_MODEL), BF16),
                        pltpu.VMEM((2, ROW_TILE, D_MODEL), F32),
                        pltpu.VMEM((ROW_TILE, D_MODEL), BF16),
                        pltpu.VMEM((2, ROW_TILE, MLP_CHUNK), BF16)],
        compiler_params=_params(1),
        name="layer1",
    )(x, g, w_in, conv_w, conv_b, w_ri, b_r, b_i, lam, w_out, mlp_g, w_up, w_down, final_g)


def kernel(x, mix_norm, mlp_norm, w_up, w_down, ab_w_in, ab_conv_w, ab_ln_g, ab_ln_b, ab_w_out,
           c_w_in, c_conv_w, c_conv_b, c_w_r, c_b_r, c_w_i, c_b_i, c_lambda, c_w_out, final_norm):
    batch, seq, d = x.shape
    vec = lambda a: a.reshape(1, -1).astype(F32)

    w_in = ab_w_in[0]
    o = 2 * CONV_CH
    w_q, w_k, w_v = (w_in[:, o + n * ATTN_WIDTH:o + (n + 1) * ATTN_WIDTH] for n in range(3))
    k, qt, vt = _qkv(x, vec(mix_norm[0]), w_k.astype(BF16), w_q.T.astype(BF16), w_v.T.astype(BF16))
    yb = _attention(qt, k, vt)
    conv_w = jnp.pad(ab_conv_w[0], ((0, 32 - CONV_WIDTH), (0, 0)))
    w_glu = w_in[:, 0:o].reshape(d, 2, CONV_CH // LANES, LANES).transpose(0, 2, 1, 3).reshape(d, o)
    x = _layer0_tail(x, yb, vec(mix_norm[0]), w_glu.astype(BF16), conv_w, vec(ab_ln_g[0]), vec(ab_ln_b[0]),
                     ab_w_out[0].astype(BF16), vec(mlp_norm[0]), w_up[0].astype(BF16), w_down[0].astype(BF16))

    w_ri = jnp.concatenate([c_w_r[0], c_w_i[0]], axis=-1).astype(BF16)
    lru_conv_w = jnp.pad(c_conv_w[0], ((0, SUBLANES - LRU_CONV_WIDTH), (0, 0)))
    return _layer1(x, vec(mix_norm[1]), c_w_in[0].astype(BF16), lru_conv_w, vec(c_conv_b[0]), w_ri,
                   vec(c_b_r[0]), vec(c_b_i[0]), vec(c_lambda[0]), c_w_out[0].astype(BF16),
                   vec(mlp_norm[1]), w_up[1].astype(BF16), w_down[1].astype(BF16), vec(final_norm))
```

```python
import functools

import jax
import jax.numpy as jnp
from jax import lax
from jax.experimental import pallas as pl
from jax.experimental.pallas import tpu as pltpu

F32 = jnp.float32
BF16 = jnp.bfloat16

EPS = 1e-6
D_MODEL = 1024
D_FF = 4 * D_MODEL
CONV_CH = 512
CONV_WIDTH = 31
ATTN_WIDTH = 512
HEAD_DIM = 64
N_HEADS = 8
MOBA_BLOCK = 256
MOBA_TOP_K = 3
LRU_HEADS = 8
LRU_BLOCK = 128
LRU_CONV_WIDTH = 4
LRU_C = 8.0

LANES = 128
SUBLANES = 8
BF16_ROWS = 16
HEADS_PER_STEP = LANES // HEAD_DIM
MASKED = -3e38
VMEM_LIMIT = 56 * 1024 * 1024

ROW_TILE = 512
QKV_ROWS = 2048
MLP_CHUNK = 1024
MLP_SUB = 256
CONV_ROWS = 128


def _dot(a, b):
    return jnp.dot(a, b, preferred_element_type=F32)


def _dot_nt(a, b):
    return lax.dot_general(a, b, (((1,), (1,)), ((), ())), preferred_element_type=F32)


def _rmsnorm(x, g):
    return x * lax.rsqrt(jnp.mean(x * x, axis=-1, keepdims=True) + EPS) * g


def _const_spec(shape):
    zeros = (0,) * len(shape)
    return pl.BlockSpec(shape, lambda *_: zeros, pipeline_mode=pl.Buffered(1))


def _params(n_axes):
    return pltpu.CompilerParams(dimension_semantics=("arbitrary",) * n_axes,
                                vmem_limit_bytes=VMEM_LIMIT)


def _qkv_kernel(x_ref, g_ref, wk_ref, wqt_ref, wvt_ref, k_ref, qt_ref, vt_ref):
    h = _rmsnorm(x_ref[...], g_ref[...]).astype(BF16)
    k_ref[...] = _dot(h, wk_ref[...]).astype(BF16)
    qt_ref[...] = _dot_nt(wqt_ref[...], h).astype(BF16)
    vt_ref[...] = _dot_nt(wvt_ref[...], h).astype(BF16)


def _qkv(x, g, wk, wqt, wvt):
    batch, seq, _ = x.shape
    w_spec = _const_spec((D_MODEL, ATTN_WIDTH))
    wt_spec = _const_spec((ATTN_WIDTH, D_MODEL))
    t_spec = pl.BlockSpec((None, ATTN_WIDTH, QKV_ROWS), lambda b, t: (b, 0, t))
    t_shape = jax.ShapeDtypeStruct((batch, ATTN_WIDTH, seq), BF16)
    return pl.pallas_call(
        _qkv_kernel,
        grid=(batch, seq // QKV_ROWS),
        in_specs=[pl.BlockSpec((None, QKV_ROWS, D_MODEL), lambda b, t: (b, t, 0)),
                  _const_spec((1, D_MODEL)), w_spec, wt_spec, wt_spec],
        out_specs=[pl.BlockSpec((None, QKV_ROWS, ATTN_WIDTH), lambda b, t: (b, t, 0)), t_spec, t_spec],
        out_shape=[jax.ShapeDtypeStruct((batch, seq, ATTN_WIDTH), BF16), t_shape, t_shape],
        compiler_params=_params(2),
        name="qkv_proj",
    )(x, g, wk, wqt, wvt)


def _head_lanes(hh):
    qk0 = hh * HEAD_DIM
    ex0 = (1 - hh) * HEAD_DIM
    return qk0, ex0, ex0 + BF16_ROWS


V_ROWS = HEAD_DIM + BF16_ROWS


def _attn_prepare(k_ref, vt_ref, slope_ref, kaug_ref, kbar_ref, vta_ref):
    n_keys = k_ref.shape[0]
    rowk = lax.broadcasted_iota(jnp.int32, (n_keys, LANES), 0)
    lane = lax.broadcasted_iota(jnp.int32, (n_keys, LANES), 1)
    tk = (rowk & (MOBA_BLOCK - 1)).astype(F32)
    blk = rowk >> (MOBA_BLOCK.bit_length() - 1)
    ones_rows = jnp.where(lax.broadcasted_iota(jnp.int32, (BF16_ROWS, n_keys), 0) == 0, 1.0, 0.0)
    for h in range(N_HEADS):
        pair, hh = divmod(h, HEADS_PER_STEP)
        qk0, ex0, bias0 = _head_lanes(hh)
        slope = slope_ref[h:h + 1, 0:LANES]
        extra = jnp.where(lane == ex0, 1.0,
                          jnp.where(lane == ex0 + 1, slope * tk,
                                    jnp.where(lane - bias0 == blk, 1.0, 0.0)))
        is_k = (lane >= qk0) & (lane < qk0 + HEAD_DIM)
        kaug_ref[h] = jnp.where(is_k, k_ref[:, pair * LANES:(pair + 1) * LANES], extra.astype(BF16))
        vta_ref[h, 0:HEAD_DIM, :] = vt_ref[h * HEAD_DIM:(h + 1) * HEAD_DIM, :]
        vta_ref[h, HEAD_DIM:V_ROWS, :] = ones_rows.astype(BF16)
    for j in range(n_keys // MOBA_BLOCK):
        kj = k_ref[j * MOBA_BLOCK:(j + 1) * MOBA_BLOCK, :].astype(F32)
        kbar_ref[j:j + 1, :] = jnp.sum(kj, axis=0, keepdims=True) * (1.0 / MOBA_BLOCK)


def _attn_block(c, qt_ref, slope_ref, kaug_ref, kbar_ref, vta_ref, o_ref):
    rowi = lax.broadcasted_iota(jnp.int32, (SUBLANES, MOBA_BLOCK), 0)
    tq = lax.broadcasted_iota(jnp.int32, (SUBLANES, MOBA_BLOCK), 1).astype(F32)
    causal = (lax.broadcasted_iota(jnp.int32, (MOBA_BLOCK, MOBA_BLOCK), 0)
              <= lax.broadcasted_iota(jnp.int32, (MOBA_BLOCK, MOBA_BLOCK), 1))
    zeros8 = jnp.zeros((SUBLANES, MOBA_BLOCK), F32)
    lane_k = lax.broadcasted_iota(jnp.int32, (SUBLANES, LANES), 1)
    n_past = c * MOBA_BLOCK
    own = slice(n_past, n_past + MOBA_BLOCK)

    qats = []
    for pair in range(N_HEADS // HEADS_PER_STEP):
        qt = qt_ref[pair * LANES:(pair + 1) * LANES, :]
        if c > 0:
            kbar = kbar_ref[:, pair * LANES:(pair + 1) * LANES]
            kbar = jnp.concatenate([jnp.where(lane_k < HEAD_DIM, kbar, 0.0),
                                    jnp.where(lane_k >= HEAD_DIM, kbar, 0.0)], axis=0)
            kb_hi = kbar.astype(BF16)
            kb_lo = (kbar - kb_hi.astype(F32)).astype(BF16)
            gates = _dot(kb_hi, qt) + _dot(kb_lo, qt)
        for hh in range(HEADS_PER_STEP):
            h = pair * HEADS_PER_STEP + hh
            qk0, ex0, bias0 = _head_lanes(hh)
            slope = slope_ref[h:h + 1, :]
            if c > 0:
                gate = gates[hh * SUBLANES:(hh + 1) * SUBLANES, :]
                elig = rowi < c
                sel = elig & (jnp.abs(gate) < jnp.inf)
                if c > MOBA_TOP_K:
                    beaten = zeros8
                    for jp in range(c):
                        gp = gate[jp:jp + 1, :]
                        beats = (gp > gate) | ((gp == gate) & (rowi > jp))
                        beaten = beaten + jnp.where(beats, 1.0, 0.0)
                    sel = sel & (beaten < float(MOBA_TOP_K))
                dist = (c - rowi).astype(F32) * float(MOBA_BLOCK)
                bias = jnp.where(elig, jnp.where(sel, -slope * dist, MASKED), 0.0)
            else:
                bias = zeros8
            ext_a = jnp.where(rowi == 0, -slope * tq, jnp.where(rowi == 1, 1.0, 0.0))
            ext = jnp.concatenate([ext_a, zeros8, bias, zeros8], axis=0).astype(BF16)
            pad = jnp.zeros((HEAD_DIM - ext.shape[0], MOBA_BLOCK), BF16)
            qs = qt[qk0:qk0 + HEAD_DIM, :] * jnp.asarray(HEAD_DIM ** -0.5, BF16)
            qats.append(jnp.concatenate([qs, ext, pad] if hh == 0 else [ext, pad, qs], axis=0))

    def logits(h):
        s_own = jnp.where(causal, _dot(kaug_ref[h, own, :], qats[h]), -jnp.inf)
        m = jnp.max(s_own, axis=0, keepdims=True)
        s_past = []
        for j in range(c):
            s_j = _dot(kaug_ref[h, j * MOBA_BLOCK:(j + 1) * MOBA_BLOCK, :], qats[h])
            m = jnp.maximum(m, jnp.max(s_j, axis=0, keepdims=True))
            s_past.append(s_j)
        return s_own, s_past, m

    def weighted_values(h, s_own, s_past, m):
        acc = _dot(vta_ref[h, :, own], jnp.exp((s_own - m).astype(BF16)))
        for j, s_j in enumerate(s_past):
            acc = acc + _dot(vta_ref[h, :, j * MOBA_BLOCK:(j + 1) * MOBA_BLOCK], jnp.exp((s_j - m).astype(BF16)))
        return acc[0:HEAD_DIM, :] * (1.0 / acc[HEAD_DIM:HEAD_DIM + 1, :])

    def store(outs):
        for pair in range(N_HEADS // HEADS_PER_STEP):
            o_pair = jnp.concatenate(outs[pair * HEADS_PER_STEP:(pair + 1) * HEADS_PER_STEP], axis=0)
            o_ref[:, pair * LANES:(pair + 1) * LANES] = o_pair.T.astype(o_ref.dtype)

    return logits, weighted_values, store


ATTN_AHEAD = 3


def _attn_kernel(slope_ref, qt_ref, k_ref, vt_ref, o_ref, kaug_ref, kbar_ref, vta_ref):
    _attn_prepare(k_ref, vt_ref, slope_ref, kaug_ref, kbar_ref, vta_ref)
    n_blk = k_ref.shape[0] // MOBA_BLOCK
    items = [(c, h) for c in range(n_blk) for h in range(N_HEADS)]
    blocks, pending, outs = {}, [], {c: [] for c in range(n_blk)}

    def issue(c, h):
        if c not in blocks:
            blk = slice(c * MOBA_BLOCK, (c + 1) * MOBA_BLOCK)
            blocks[c] = _attn_block(c, qt_ref.at[:, blk], slope_ref, kaug_ref, kbar_ref, vta_ref, o_ref.at[blk])
        pending.append(blocks[c][0](h))

    for c, h in items[:ATTN_AHEAD]:
        issue(c, h)
    for n, (c, h) in enumerate(items):
        if n + ATTN_AHEAD < len(items):
            issue(*items[n + ATTN_AHEAD])
        outs[c].append(blocks[c][1](h, *pending.pop(0)))
        if h == N_HEADS - 1:
            blocks[c][2](outs.pop(c))


def _attention(qt, k, vt):
    batch, seq, _ = k.shape
    n_blk = seq // MOBA_BLOCK
    slopes = 2.0 ** (-8.0 * jnp.arange(1, N_HEADS + 1, dtype=F32) / N_HEADS)
    slopes = jnp.broadcast_to(slopes.reshape(N_HEADS, 1), (N_HEADS, MOBA_BLOCK))
    return pl.pallas_call(
        _attn_kernel,
        grid=(batch,),
        in_specs=[
            _const_spec((N_HEADS, MOBA_BLOCK)),
            pl.BlockSpec((None, ATTN_WIDTH, seq), lambda b: (b, 0, 0)),
            pl.BlockSpec((None, seq, ATTN_WIDTH), lambda b: (b, 0, 0)),
            pl.BlockSpec((None, ATTN_WIDTH, seq), lambda b: (b, 0, 0)),
        ],
        out_specs=pl.BlockSpec((None, seq, ATTN_WIDTH), lambda b: (b, 0, 0)),
        out_shape=jax.ShapeDtypeStruct((batch, seq, ATTN_WIDTH), BF16),
        scratch_shapes=[pltpu.VMEM((N_HEADS, seq, LANES), BF16),
                        pltpu.VMEM((n_blk, ATTN_WIDTH), F32),
                        pltpu.VMEM((N_HEADS, V_ROWS, seq), BF16)],
        compiler_params=_params(1),
        name="moba_attention",
    )(slopes, qt, k, vt)


def _to_time_major(val, buf_ref, slab0, row0, batch, t_blk):
    for s in range(val.shape[1] // LANES):
        for b in range(batch):
            buf_ref[slab0 + s, pl.ds(row0 + b, t_blk, stride=batch), :] = (
                val[b * t_blk:(b + 1) * t_blk, s * LANES:(s + 1) * LANES])


def _from_time_major(buf_ref, s, b, batch, t_blk):
    return buf_ref[s, pl.ds(b, t_blk, stride=batch), :]


def _mix0_kernel(x_ref, yb_ref, g_ref, wz_ref, cw_ref, lg_ref, lb_ref, w_ref, mg_ref, wu_ref, wd_ref,
                 o_ref, zbuf_ref, conv_ref, lhs_ref, xmid_ref, hm_ref, act_ref):
    i = pl.program_id(0)
    batch, t_blk, _ = x_ref.shape
    rows = batch * t_blk
    n_slab = CONV_CH // LANES
    slot = i % 2

    @pl.when(i == 0)
    def _():
        zbuf_ref[:, 0:rows, :] = jnp.zeros((n_slab, rows, LANES), F32)
        xmid_ref[1] = jnp.zeros((rows, D_MODEL), F32)

    @pl.when(i > 0)
    def _():
        zbuf_ref[:, 0:rows, :] = zbuf_ref[:, rows:2 * rows, :]

    x = x_ref[...].reshape(rows, D_MODEL)
    h = _rmsnorm(x, g_ref[...]).astype(BF16)

    glu_in = {}
    stats = {}
    mine = xmid_ref.at[slot]

    def project(s):
        glu_in[s] = _dot(h, wz_ref[:, 2 * s * LANES:2 * (s + 1) * LANES])

    tokens = {}

    def conv(s):
        u = glu_in.pop(s)
        z = u[:, 0:LANES] * jax.nn.sigmoid(u[:, LANES:2 * LANES])
        _to_time_major(z, zbuf_ref, s, rows, batch, t_blk)
        for r0 in range(0, rows, CONV_ROWS):
            q = s * (rows // CONV_ROWS) + r0 // CONV_ROWS
            tok = tokens[1 + (4 * q) // 3]
            zero = jnp.where(jnp.abs(tok) < jnp.inf, tok - tok, 0.0)
            acc = jnp.concatenate([zero] * (CONV_ROWS // SUBLANES), axis=0)
            for j in range(CONV_WIDTH):
                start = r0 + rows - (CONV_WIDTH - 1 - j) * batch
                acc = acc + zbuf_ref[s, start:start + CONV_ROWS, :] * cw_ref[j:j + 1, s * LANES:(s + 1) * LANES]
            conv_ref[s, r0:r0 + CONV_ROWS, :] = acc

    def attention_part(j):
        cols = slice(j * CONV_CH, (j + 1) * CONV_CH)
        yb = yb_ref[...].reshape(rows, ATTN_WIDTH)
        mine[:, cols] = x_ref[:, :, cols].reshape(rows, CONV_CH) + _dot(yb, w_ref[CONV_CH:CONV_CH + ATTN_WIDTH, cols])

    def layernorm_stats():
        y = [conv_ref[s] for s in range(n_slab)]
        mu = jnp.sum(y[0] + y[1] + y[2] + y[3], axis=-1, keepdims=True) * (1.0 / CONV_CH)
        d = [ys - mu for ys in y]
        var = jnp.sum(d[0] * d[0] + d[1] * d[1] + d[2] * d[2] + d[3] * d[3], axis=-1, keepdims=True) * (1.0 / CONV_CH)
        stats["mu"], stats["inv"] = mu, lax.rsqrt(var + EPS)

    def normalize(s):
        cols = slice(s * LANES, (s + 1) * LANES)
        yn = (conv_ref[s] - stats["mu"]) * stats["inv"] * lg_ref[:, cols] + lb_ref[:, cols]
        conv_ref[s] = yn * jax.nn.sigmoid(yn)
        for b in range(batch):
            lhs_ref[b * t_blk:(b + 1) * t_blk, cols] = _from_time_major(conv_ref, s, b, batch, t_blk).astype(BF16)

    def conv_part(first):
        k_rows = slice(first * LANES, (first + 2) * LANES)
        mine[...] = mine[...] + _dot(lhs_ref[:, k_rows], w_ref[k_rows, :])

    mlp = _mlp_stages(xmid_ref.at[1 - slot], mg_ref, wu_ref, wd_ref, None, o_ref, hm_ref, act_ref)
    mlp.norm()
    for c in range(mlp.n_chunks):
        for stage in (mlp.up, mlp.down):
            for j in range(mlp.n_sub):
                tokens[len(tokens)] = stage(c, j)
    for s in range(n_slab):
        project(s)
        conv(s)
    attention_part(0)
    attention_part(1)
    layernorm_stats()
    for first in range(0, n_slab, 2):
        normalize(first)
        normalize(first + 1)
        conv_part(first)


def _layer0_tail(x, yb, g, w_z, conv_w, ln_g, ln_b, w_out, mlp_g, w_up, w_down):
    batch, seq, _ = x.shape
    t_blk = ROW_TILE // batch
    n_blk = seq // t_blk
    assert (CONV_WIDTH - 1) * batch <= ROW_TILE and CONV_CH == 4 * LANES
    blk = lambda width: pl.BlockSpec((batch, t_blk, width), lambda i: (0, jnp.minimum(i, n_blk - 1), 0))
    vec = _const_spec((1, CONV_CH))
    return pl.pallas_call(
        _mix0_kernel,
        grid=(n_blk + 1,),
        in_specs=[blk(D_MODEL), blk(ATTN_WIDTH), _const_spec((1, D_MODEL)), _const_spec(w_z.shape),
                  _const_spec(conv_w.shape), vec, vec, _const_spec(w_out.shape),
                  _const_spec((1, D_MODEL)), _const_spec(w_up.shape), _const_spec(w_down.shape)],
        out_specs=pl.BlockSpec((batch, t_blk, D_MODEL), lambda i: (0, jnp.maximum(i - 1, 0), 0)),
        out_shape=jax.ShapeDtypeStruct(x.shape, F32),
        scratch_shapes=[pltpu.VMEM((CONV_CH // LANES, 2 * ROW_TILE, LANES), F32),
                        pltpu.VMEM((CONV_CH // LANES, ROW_TILE, LANES), F32),
                        pltpu.VMEM((ROW_TILE, CONV_CH), BF16),
                        pltpu.VMEM((2, ROW_TILE, D_MODEL), F32),
                        pltpu.VMEM((ROW_TILE, D_MODEL), BF16),
                        pltpu.VMEM((2, ROW_TILE, MLP_CHUNK), BF16)],
        compiler_params=_params(1),
        name="layer0_tail",
    )(x, yb, g, w_z, conv_w, ln_g, ln_b, w_out, mlp_g, w_up, w_down)


class _mlp_stages:
    n_chunks = D_FF // MLP_CHUNK
    n_sub = MLP_CHUNK // MLP_SUB

    def __init__(self, x_ref, g_ref, wu_ref, wd_ref, final_g_ref, o_ref, h_ref, act_ref):
        self.x_ref, self.g_ref, self.wu_ref, self.wd_ref = x_ref, g_ref, wu_ref, wd_ref
        self.final_g_ref, self.o_ref, self.h_ref, self.act_ref = final_g_ref, o_ref, h_ref, act_ref

    def norm(self):
        self.h_ref[...] = _rmsnorm(self.x_ref[...], self.g_ref[...]).astype(BF16)

    def up(self, c, j):
        cols = slice(j * MLP_SUB, (j + 1) * MLP_SUB)
        a = _dot(self.h_ref[...], self.wu_ref[:, c * MLP_CHUNK + j * MLP_SUB:c * MLP_CHUNK + (j + 1) * MLP_SUB])
        self.act_ref[c % 2, :, cols] = jnp.square(jnp.maximum(a, 0.0)).astype(BF16)
        return self.act_ref[c % 2, 0:BF16_ROWS, j * MLP_SUB:j * MLP_SUB + LANES].astype(F32)[0:SUBLANES]

    def down(self, c, j):
        cols = slice(j * MLP_SUB, (j + 1) * MLP_SUB)
        batch, t_blk, _ = self.o_ref.shape
        part = _dot(self.act_ref[c % 2], self.wd_ref[c * MLP_CHUNK:(c + 1) * MLP_CHUNK, cols])
        acc = self.x_ref[:, cols] if c == 0 else self.o_ref[:, :, cols].reshape(part.shape)
        self.o_ref[:, :, cols] = (acc + part).reshape(batch, t_blk, MLP_SUB)
        return self.o_ref[0, 0:SUBLANES, j * MLP_SUB:j * MLP_SUB + LANES]

    def finish(self):
        if self.final_g_ref is not None:
            shape = self.o_ref.shape
            acc = self.o_ref[...].reshape(shape[0] * shape[1], shape[2])
            self.o_ref[...] = _rmsnorm(acc, self.final_g_ref[...]).reshape(shape)


def _gelu_tanh(x):
    return 0.5 * x * (1.0 + jnp.tanh(0.7978845608028654 * (x + 0.044715 * (x * x * x))))


def _lru_kernel(x_ref, g_ref, win_ref, cw_ref, cb_ref, wri_ref, br_ref, bi_ref, lam_ref, wout_ref,
                mg_ref, wu_ref, wd_ref, fg_ref,
                o_ref, xbuf_ref, a_ref, b_ref, gate_ref, h_ref, y_ref, xmid_ref, hm_ref, act_ref):
    i = pl.program_id(0)
    batch, t_blk, _ = x_ref.shape
    rows = batch * t_blk
    halo = xbuf_ref.shape[1] - rows
    slot = i % 2

    @pl.when(i == 0)
    def _():
        xbuf_ref[:, 0:halo, :] = jnp.zeros((LRU_HEADS, halo, LANES), F32)
        h_ref[...] = jnp.zeros(h_ref.shape, F32)
        xmid_ref[1] = jnp.zeros((rows, D_MODEL), F32)

    @pl.when(i > 0)
    def _():
        xbuf_ref[:, 0:halo, :] = xbuf_ref[:, rows:rows + halo, :]

    x = x_ref[...].reshape(rows, D_MODEL)
    h = _rmsnorm(x, g_ref[...]).astype(BF16)
    neg_lam = -lam_ref[...]
    softplus = jnp.maximum(neg_lam, 0.0) + jnp.log1p(jnp.exp(-jnp.abs(neg_lam)))
    group = 2
    width = group * LRU_BLOCK

    def project_gate(p):
        cols = slice(p * width, (p + 1) * width)
        gate_ref[:, cols] = _gelu_tanh(_dot(h, win_ref[:, cols]))

    def project_x(p):
        xr = _dot(h, win_ref[:, D_MODEL + p * width:D_MODEL + (p + 1) * width])
        _to_time_major(xr, xbuf_ref, p * group, halo, batch, t_blk)

    def recurrence_inputs(hd):
        cols = slice(hd * LRU_BLOCK, (hd + 1) * LRU_BLOCK)
        xc = cb_ref[:, cols]
        for j in range(LRU_CONV_WIDTH):
            start = halo - (LRU_CONV_WIDTH - 1 - j) * batch
            xc = xc + xbuf_ref[hd, start:start + rows, :] * cw_ref[j:j + 1, cols]
        ri = _dot(xc.astype(BF16), wri_ref[hd])
        r = jax.nn.sigmoid(ri[:, 0:LRU_BLOCK] + br_ref[:, cols])
        ig = jax.nn.sigmoid(ri[:, LRU_BLOCK:2 * LRU_BLOCK] + bi_ref[:, cols])
        log_a = -LRU_C * r * softplus[:, cols]
        a = jnp.exp(log_a)
        a_ref[hd] = a
        b_ref[hd] = jnp.sqrt(-jnp.tanh(log_a) * (a * a + 1.0)) * (ig * xc)

    def scan(p):
        heads = range(p * group, (p + 1) * group)
        state = {hd: h_ref[hd] for hd in heads}
        for t in range(t_blk):
            step = slice(t * batch, (t + 1) * batch)
            for hd in heads:
                state[hd] = a_ref[hd, step, :] * state[hd] + b_ref[hd, step, :]
                b_ref[hd, step, :] = state[hd]
        for hd in heads:
            h_ref[hd] = state[hd]
            cols = slice(hd * LRU_BLOCK, (hd + 1) * LRU_BLOCK)
            for b in range(batch):
                blk = slice(b * t_blk, (b + 1) * t_blk)
                y_ref[blk, cols] = (_from_time_major(b_ref, hd, b, batch, t_blk) * gate_ref[blk, cols]).astype(BF16)

    def out_part(p):
        k_rows = slice(p * width, (p + 1) * width)
        return _dot(y_ref[:, k_rows], wout_ref[k_rows, :])

    n_groups = LRU_HEADS // group
    mlp = _mlp_stages(xmid_ref.at[1 - slot], mg_ref, wu_ref, wd_ref, fg_ref, o_ref, hm_ref, act_ref)
    assert n_groups == mlp.n_chunks and mlp.n_sub == 4 and group == 2
    out = [x]

    def add_out_part(p):
        out[0] = out[0] + out_part(p)

    mlp.norm()
    project_x(0)
    project_gate(0)
    for p in range(n_groups):
        last = p + 1 == n_groups
        mixer = [lambda: recurrence_inputs(p * group),
                 lambda: recurrence_inputs(p * group + 1),
                 (lambda: None) if last else (lambda: project_x(p + 1)),
                 (lambda: None) if last else (lambda: project_gate(p + 1)),
                 lambda: scan(p),
                 (lambda: add_out_part(p - 1)) if p > 0 else (lambda: None)]
        dense = [lambda j=j: mlp.up(p, j) for j in range(mlp.n_sub)]
        dense += [lambda j=j: mlp.down(p, j) for j in range(mlp.n_sub)]
        for k, dense_stage in enumerate(dense):
            dense_stage()
            if k < len(mixer):
                mixer[k]()
    xmid_ref[slot] = out[0] + out_part(n_groups - 1)
    mlp.finish()


def _layer1(x, g, w_in, conv_w, conv_b, w_ri, b_r, b_i, lam, w_out, mlp_g, w_up, w_down, final_g):
    batch, seq, _ = x.shape
    t_blk = ROW_TILE // batch
    n_blk = seq // t_blk
    halo = 64
    assert (LRU_CONV_WIDTH - 1) * batch <= halo and LRU_BLOCK == LANES
    vec = _const_spec((1, D_MODEL))
    slab = lambda n: pltpu.VMEM((LRU_HEADS, n, LANES), F32)
    return pl.pallas_call(
        _lru_kernel,
        grid=(n_blk + 1,),
        in_specs=[pl.BlockSpec((batch, t_blk, D_MODEL), lambda i: (0, jnp.minimum(i, n_blk - 1), 0)),
                  vec, _const_spec(w_in.shape), _const_spec(conv_w.shape), vec,
                  _const_spec(w_ri.shape), vec, vec, vec, _const_spec(w_out.shape),
                  vec, _const_spec(w_up.shape), _const_spec(w_down.shape), vec],
        out_specs=pl.BlockSpec((batch, t_blk, D_MODEL), lambda i: (0, jnp.maximum(i - 1, 0), 0)),
        out_shape=jax.ShapeDtypeStruct(x.shape, F32),
        scratch_shapes=[slab(halo + ROW_TILE), slab(ROW_TILE), slab(ROW_TILE),
                        pltpu.VMEM((ROW_TILE, D_MODEL), F32), slab(batch),
                        pltpu.VMEM((ROW_TILE, D_MODEL), BF16),
                        pltpu.VMEM((2, ROW_TILE, D_MODEL), F32),
                        pltpu.VMEM((ROW_TILE, D_MODEL), BF16),
                        pltpu.VMEM((2, ROW_TILE, MLP_CHUNK), BF16)],
        compiler_params=_params(1),
        name="layer1",
    )(x, g, w_in, conv_w, conv_b, w_ri, b_r, b_i, lam, w_out, mlp_g, w_up, w_down, final_g)


def kernel(x, mix_norm, mlp_norm, w_up, w_down, ab_w_in, ab_conv_w, ab_ln_g, ab_ln_b, ab_w_out,
           c_w_in, c_conv_w, c_conv_b, c_w_r, c_b_r, c_w_i, c_b_i, c_lambda, c_w_out, final_norm):
    batch, seq, d = x.shape
    vec = lambda a: a.reshape(1, -1).astype(F32)

    w_in = ab_w_in[0]
    o = 2 * CONV_CH
    w_q, w_k, w_v = (w_in[:, o + n * ATTN_WIDTH:o + (n + 1) * ATTN_WIDTH] for n in range(3))
    k, qt, vt = _qkv(x, vec(mix_norm[0]), w_k.astype(BF16), w_q.T.astype(BF16), w_v.T.astype(BF16))
    yb = _attention(qt, k, vt)
    conv_w = jnp.pad(ab_conv_w[0], ((0, 32 - CONV_WIDTH), (0, 0)))
    w_glu = w_in[:, 0:o].reshape(d, 2, CONV_CH // LANES, LANES).transpose(0, 2, 1, 3).reshape(d, o)
    x = _layer0_tail(x, yb, vec(mix_norm[0]), w_glu.astype(BF16), conv_w, vec(ab_ln_g[0]), vec(ab_ln_b[0]),
                     ab_w_out[0].astype(BF16), vec(mlp_norm[0]), w_up[0].astype(BF16), w_down[0].astype(BF16))

    w_ri = jnp.concatenate([c_w_r[0], c_w_i[0]], axis=-1).astype(BF16)
    lru_conv_w = jnp.pad(c_conv_w[0], ((0, SUBLANES - LRU_CONV_WIDTH), (0, 0)))
    return _layer1(x, vec(mix_norm[1]), c_w_in[0].astype(BF16), lru_conv_w, vec(c_conv_b[0]), w_ri,
                   vec(c_b_r[0]), vec(c_b_i[0]), vec(c_lambda[0]), c_w_out[0].astype(BF16),
                   vec(mlp_norm[1]), w_up[1].astype(BF16), w_down[1].astype(BF16), vec(final_norm))
```
